```python
import jax, jax.numpy as jnp
from jax import lax
import numpy as np

D_MODEL = 1024
BATCH = 32
SEQ = 2048
DEPTH = 2
DEC_BATCH = 16
DEC_SEQ = 64
PAST_LEN = 2048

CHUNK = 64
HEAD_DIM = 64
N_HEADS = (D_MODEL // 2) // HEAD_DIM
ATT_DIM = N_HEADS * HEAD_DIM
CONV_DIM = D_MODEL // 2
CONV_W = 3
MIX_DIM = ATT_DIM + CONV_DIM
IN_DIM = 4 * ATT_DIM + N_HEADS + 4 * CONV_DIM
SPLIT_POINTS = (ATT_DIM, 2 * ATT_DIM, 3 * ATT_DIM, 4 * ATT_DIM, 4 * ATT_DIM + N_HEADS,
                4 * ATT_DIM + N_HEADS + CONV_DIM, 4 * ATT_DIM + N_HEADS + 2 * CONV_DIM,
                4 * ATT_DIM + N_HEADS + 3 * CONV_DIM)
Q_BLOCK = 128
ATTN_SCALE = HEAD_DIM ** -0.5
EPS = 1e-6
NEG = -1e30

kernel_name = "fox_shortconv_parallel_hybrid_step"


def rmsnorm(x, g):
    xf = x.astype(jnp.float32)
    y = xf * lax.rsqrt(jnp.mean(xf * xf, axis=-1, keepdims=True) + EPS)
    return (y * g.astype(jnp.float32)).astype(x.dtype)


def fox_attention(q, k, v, c_q, c_k, q_pos, k_pos):
    s = jnp.einsum('bqhd,bkhd->bhqk', q, k, preferred_element_type=jnp.float32) * ATTN_SCALE
    decay = jnp.transpose(c_q, (0, 2, 1))[..., :, None] - jnp.transpose(c_k, (0, 2, 1))[..., None, :]
    s = jnp.where(k_pos[None, :] <= q_pos[:, None], s + decay, NEG)
    p = jax.nn.softmax(s, axis=-1)
    return jnp.einsum('bhqk,bkhd->bqhd', p.astype(v.dtype), v)


def mixer_layer(x, norm_g, w_in, b_f, q_g, k_g, conv_w, att_g, conv_g, w_out,
                past_k=None, past_v=None, past_logf=None, past_conv=None):
    b, n, _ = x.shape
    h = rmsnorm(x, norm_g)
    proj = h @ w_in
    q, k, v, g_a, f_pre, cb, cc, ch, g_c = jnp.split(proj, SPLIT_POINTS, axis=-1)
    q = rmsnorm(q.reshape(b, n, N_HEADS, HEAD_DIM), q_g)
    k = rmsnorm(k.reshape(b, n, N_HEADS, HEAD_DIM), k_g)
    v = v.reshape(b, n, N_HEADS, HEAD_DIM)
    logf = jax.nn.log_sigmoid((f_pre + b_f).astype(jnp.float32))

    if past_k is None:
        c = jnp.cumsum(logf, axis=1)
        outs = []
        for i in range(n // Q_BLOCK):
            s0, e = i * Q_BLOCK, (i + 1) * Q_BLOCK
            outs.append(fox_attention(q[:, s0:e], k[:, :e], v[:, :e], c[:, s0:e], c[:, :e],
                                      jnp.arange(s0, e), jnp.arange(e)))
        att = jnp.concatenate(outs, axis=1)
        conv_past = jnp.zeros((b, CONV_W - 1, CONV_DIM), x.dtype)
    else:
        p_len = past_k.shape[1]
        k_all = jnp.concatenate([past_k.astype(k.dtype), k], axis=1)
        v_all = jnp.concatenate([past_v.astype(v.dtype), v], axis=1)
        c = jnp.cumsum(jnp.concatenate([past_logf.astype(jnp.float32), logf], axis=1), axis=1)
        att = fox_attention(q, k_all, v_all, c[:, p_len:], c,
                            p_len + jnp.arange(n), jnp.arange(p_len + n))
        conv_past = past_conv.astype(x.dtype)

    u = cc * ch
    u_pad = jnp.concatenate([conv_past, u], axis=1)
    y = sum(conv_w[i] * u_pad[:, i:i + n] for i in range(CONV_W))
    z = cb * y

    att = att.reshape(b, n, ATT_DIM)
    mix = jnp.concatenate([rmsnorm(att, att_g) * jax.nn.silu(g_a),
                           rmsnorm(z, conv_g) * jax.nn.silu(g_c)], axis=-1)
    out = x + mix @ w_out
    return out, k, v, logf.astype(x.dtype), u_pad[:, -(CONV_W - 1):]


def setup_inputs(seed: int = 0) -> dict:
    key = jax.random.key(seed)
    ks = jax.random.split(key, 16)
    f32 = jnp.float32
    x_prompt = jax.random.normal(ks[0], (BATCH, SEQ, D_MODEL), f32)
    x_sample = jax.random.normal(ks[1], (DEC_BATCH, DEC_SEQ, D_MODEL), f32)
    cache_k = jax.random.normal(ks[2], (DEPTH, DEC_BATCH, PAST_LEN, N_HEADS, HEAD_DIM), f32)
    cache_v = jax.random.normal(ks[3], (DEPTH, DEC_BATCH, PAST_LEN, N_HEADS, HEAD_DIM), f32)
    cache_logf = jax.nn.log_sigmoid(3.0 + jax.random.normal(ks[4], (DEPTH, DEC_BATCH, PAST_LEN, N_HEADS), f32))
    state_conv = 0.5 * jax.random.normal(ks[5], (DEPTH, DEC_BATCH, CONV_W - 1, CONV_DIM), f32)
    norm_g = 1.0 + 0.01 * jax.random.normal(ks[6], (DEPTH, D_MODEL), f32)
    w_in = jax.random.normal(ks[7], (DEPTH, D_MODEL, IN_DIM), f32) * D_MODEL ** -0.5
    b_f = jax.random.uniform(ks[8], (DEPTH, N_HEADS), f32, minval=1.0, maxval=6.0)
    q_norm_g = 1.0 + 0.01 * jax.random.normal(ks[9], (DEPTH, HEAD_DIM), f32)
    k_norm_g = 1.0 + 0.01 * jax.random.normal(ks[10], (DEPTH, HEAD_DIM), f32)
    conv_w = jax.random.normal(ks[11], (DEPTH, CONV_W, CONV_DIM), f32) * CONV_W ** -0.5
    att_out_g = 1.0 + 0.01 * jax.random.normal(ks[12], (DEPTH, ATT_DIM), f32)
    conv_out_g = 1.0 + 0.01 * jax.random.normal(ks[13], (DEPTH, CONV_DIM), f32)
    w_out = jax.random.normal(ks[14], (DEPTH, MIX_DIM, D_MODEL), f32) * (0.5 * MIX_DIM ** -0.5)
    return {"x_prompt": x_prompt, "x_sample": x_sample, "cache_k": cache_k, "cache_v": cache_v,
            "cache_logf": cache_logf, "state_conv": state_conv, "norm_g": norm_g, "w_in": w_in,
            "b_f": b_f, "q_norm_g": q_norm_g, "k_norm_g": k_norm_g, "conv_w": conv_w,
            "att_out_g": att_out_g, "conv_out_g": conv_out_g, "w_out": w_out}


def reference(x_prompt, x_sample, cache_k, cache_v, cache_logf, state_conv, norm_g, w_in, b_f,
              q_norm_g, k_norm_g, conv_w, att_out_g, conv_out_g, w_out):
    yp, ys = x_prompt, x_sample
    kp, vp, fp, cp, ksm, vsm, fsm, csm = [], [], [], [], [], [], [], []
    for l in range(DEPTH):
        params = (norm_g[l], w_in[l], b_f[l], q_norm_g[l], k_norm_g[l], conv_w[l],
                  att_out_g[l], conv_out_g[l], w_out[l])
        yp, k1, v1, f1, c1 = mixer_layer(yp, *params)
        ys, k2, v2, f2, c2 = mixer_layer(ys, *params, cache_k[l], cache_v[l], cache_logf[l], state_conv[l])
        kp.append(k1); vp.append(v1); fp.append(f1); cp.append(c1)
        ksm.append(k2); vsm.append(v2); fsm.append(f2); csm.append(c2)
    return (yp, ys, jnp.stack(kp), jnp.stack(vp), jnp.stack(fp), jnp.stack(cp),
            jnp.stack(ksm), jnp.stack(vsm), jnp.stack(fsm), jnp.stack(csm))
```

```python
import functools

import numpy as np
import jax
import jax.numpy as jnp
from jax import lax
from jax.experimental import pallas as pl
from jax.experimental.pallas import tpu as pltpu

D_MODEL = 1024
HEAD_DIM = 64
N_HEADS = 8
N_PAIRS = N_HEADS // 2
ATT_DIM = N_HEADS * HEAD_DIM
CONV_DIM = 512
CONV_W = 3
EPS = 1e-6
NEG = -1e30
ATTN_SCALE = HEAD_DIM ** -0.5

LANES = 128
SUBLANES = 8
F_PAD = LANES
OFF_Q, OFF_K, OFF_V, OFF_GA = 0, 512, 1024, 1536
OFF_CB, OFF_CC, OFF_CH, OFF_GC = 2048, 2560, 3072, 3584
OFF_F = 4096
IN_PAD = OFF_F + F_PAD

SEQ_TILE = 256
PAST_BLK = 256
VMEM_LIMIT = 56 * 1024 * 1024

BF16 = jnp.bfloat16
F32 = jnp.float32


def _dot(a, b):
    return jnp.dot(a, b, preferred_element_type=F32)


def _dot_nt(a, b):
    return lax.dot_general(a, b, (((1,), (1,)), ((), ())), preferred_element_type=F32)


def _split3(x):
    hi = x.astype(BF16)
    r = x - hi.astype(F32)
    mid = r.astype(BF16)
    lo = (r - mid.astype(F32)).astype(BF16)
    return hi, mid, lo


def _dot_exact_rhs(m01, x):
    hi, mid, lo = _split3(x)
    return _dot(m01, hi) + _dot(m01, mid) + _dot(m01, lo)


def _dot_exact_lhs(x, m01):
    hi, mid, lo = _split3(x)
    return _dot(hi, m01) + _dot(mid, m01) + _dot(lo, m01)


def _rms(x, g):
    ms = jnp.mean(x * x, axis=-1, keepdims=True)
    return x * lax.rsqrt(ms + EPS) * g


def _head_rms(x, g_tiled, bd):
    ss = _dot((x * x).astype(BF16), bd)
    return x * lax.rsqrt(ss * (1.0 / HEAD_DIM) + EPS) * g_tiled


def _silu(x):
    return x / (1.0 + jnp.exp(-x))


def _log_sigmoid(x):
    return jnp.minimum(x, 0.0) - jnp.log1p(jnp.exp(-jnp.abs(x)))


def _shift_rows(u, prev8, k):
    r = pltpu.roll(u, k, axis=0)
    pr = pltpu.roll(prev8, k, axis=0)
    rowid = lax.broadcasted_iota(jnp.int32, prev8.shape, 0)
    head = jnp.where(rowid < k, pr, r[0:SUBLANES])
    return jnp.concatenate([head, r[SUBLANES:]], axis=0)


def _front(x, ng_ref, w_ref, bf_ref, qg_ref, kg_ref, bd_ref):
    h = _rms(x, ng_ref[...]).astype(BF16)
    bd = bd_ref[...]
    q = _head_rms(_dot(h, w_ref[:, OFF_Q:OFF_Q + ATT_DIM]), qg_ref[...], bd) * ATTN_SCALE
    k = _head_rms(_dot(h, w_ref[:, OFF_K:OFF_K + ATT_DIM]), kg_ref[...], bd)
    v = _dot(h, w_ref[:, OFF_V:OFF_V + ATT_DIM])
    logf = _log_sigmoid(_dot(h, w_ref[:, OFF_F:OFF_F + F_PAD]) + bf_ref[...])
    return h, q, k, v, logf


def _masked_queries(q):
    t = q.shape[0]
    low = lax.broadcasted_iota(jnp.int32, (t, LANES), 1) < HEAD_DIM
    out = []
    for hd in range(N_HEADS):
        p = hd // 2
        blk = q[:, p * LANES:(p + 1) * LANES]
        keep = low if hd % 2 == 0 else jnp.logical_not(low)
        out.append(jnp.where(keep, blk, 0.0).astype(BF16))
    return out


def _merge_heads(outs):
    t = outs[0].shape[0]
    low = lax.broadcasted_iota(jnp.int32, (t, LANES), 1) < HEAD_DIM
    cols = []
    for p in range(N_PAIRS):
        a0, a1 = outs[2 * p], outs[2 * p + 1]
        o0 = a0[:, :LANES] / a0[:, LANES:]
        o1 = a1[:, :LANES] / a1[:, LANES:]
        cols.append(jnp.where(low, o0, o1))
    return jnp.concatenate(cols, axis=1)


def _back(x, h, att, u_prev8, w_ref, cw_ref, ag_ref, cg_ref, wo_ref):
    cc = _dot(h, w_ref[:, OFF_CC:OFF_CC + CONV_DIM])
    ch = _dot(h, w_ref[:, OFF_CH:OFF_CH + CONV_DIM])
    u = cc * ch
    cw = cw_ref[...]
    yc = cw[0:1] * _shift_rows(u, u_prev8, 2) + cw[1:2] * _shift_rows(u, u_prev8, 1) + cw[2:3] * u
    z = _dot(h, w_ref[:, OFF_CB:OFF_CB + CONV_DIM]) * yc
    zn = _rms(z, cg_ref[...]) * _silu(_dot(h, w_ref[:, OFF_GC:OFF_GC + CONV_DIM]))
    an = _rms(att, ag_ref[...]) * _silu(_dot(h, w_ref[:, OFF_GA:OFF_GA + ATT_DIM]))
    mix = jnp.concatenate([an, zn], axis=1).astype(BF16)
    return x + _dot(mix, wo_ref[...]), u


def _prompt_kernel(x_ref, ng_ref, w_ref, bf_ref, qg_ref, kg_ref, cw_ref, ag_ref, cg_ref, wo_ref,
                   bd_ref, tri_ref,
                   y_ref, ko_ref, vo_ref, fo_ref, co_ref,
                   kt_ref, vx_ref, ck_ref, qm_ref, m_ref, acc_ref, carry_ref, uprev_ref):
    i = pl.program_id(1)
    t = SEQ_TILE

    @pl.when(i == 0)
    def _():
        carry_ref[...] = jnp.zeros_like(carry_ref)
        uprev_ref[...] = jnp.zeros_like(uprev_ref)

    x = x_ref[0]
    h, q, k, v, logf = _front(x, ng_ref, w_ref, bf_ref, qg_ref, kg_ref, bd_ref)
    ko_ref[0] = k
    vo_ref[0] = v
    fo_ref[0] = logf[:, :N_HEADS]

    c = _dot_exact_rhs(tri_ref[...], logf) + carry_ref[...]
    carry_ref[...] = c[t - 1:t, :]
    ck_ref[i] = c.T[:N_HEADS, :]

    ones = jnp.ones((t, LANES), BF16)
    for p in range(N_PAIRS):
        kt_ref[p, i] = k[:, p * LANES:(p + 1) * LANES].T.astype(BF16)
        vx_ref[p, i] = jnp.concatenate([v[:, p * LANES:(p + 1) * LANES].astype(BF16), ones], axis=1)
    for hd, qm in enumerate(_masked_queries(q)):
        qm_ref[hd] = qm

    m_ref[...] = jnp.full(m_ref.shape, NEG, F32)
    acc_ref[...] = jnp.zeros_like(acc_ref)

    def step(j, masked):
        if masked:
            row = lax.broadcasted_iota(jnp.int32, (t, t), 0)
            col = lax.broadcasted_iota(jnp.int32, (t, t), 1)
            keep = col <= row
        for hd in range(N_HEADS):
            p = hd // 2
            z = _dot(qm_ref[hd], kt_ref[p, j]) - ck_ref[j, hd:hd + 1, :]
            if masked:
                z = jnp.where(keep, z, NEG)
            m_old = m_ref[hd]
            m_new = jnp.maximum(m_old, jnp.max(z, axis=-1, keepdims=True))
            pr = jnp.exp(z - m_new).astype(BF16)
            acc_ref[hd] = jnp.exp(m_old - m_new) * acc_ref[hd] + _dot(pr, vx_ref[p, j])
            m_ref[hd] = m_new

    def body(j, carry):
        step(j, False)
        return carry

    lax.fori_loop(0, i, body, 0)
    step(i, True)

    att = _merge_heads([acc_ref[hd] for hd in range(N_HEADS)])
    y, u = _back(x, h, att, uprev_ref[...], w_ref, cw_ref, ag_ref, cg_ref, wo_ref)
    y_ref[0] = y
    uprev_ref[...] = u[t - SUBLANES:, :]
    co_ref[0] = u[t - (CONV_W - 1):, :]


def _const_spec(shape):
    nd = len(shape)
    return pl.BlockSpec(shape, lambda *_: (0,) * nd)


def _prompt_layer(x, ng, w, bfp, qg, kg, cw, ag, cg, wo, bd, tri):
    b, s, d = x.shape
    t = SEQ_TILE
    nt = s // t
    consts = (ng, w, bfp, qg, kg, cw, ag, cg, wo, bd, tri)
    tile3 = lambda width: pl.BlockSpec((1, t, width), lambda bi, ti: (bi, ti, 0))
    return pl.pallas_call(
        _prompt_kernel,
        grid=(b, nt),
        in_specs=[tile3(d)] + [_const_spec(a.shape) for a in consts],
        out_specs=[tile3(d), tile3(ATT_DIM), tile3(ATT_DIM), tile3(N_HEADS),
                   pl.BlockSpec((1, CONV_W - 1, CONV_DIM), lambda bi, ti: (bi, 0, 0))],
        out_shape=[jax.ShapeDtypeStruct((b, s, d), F32),
                   jax.ShapeDtypeStruct((b, s, ATT_DIM), F32),
                   jax.ShapeDtypeStruct((b, s, ATT_DIM), F32),
                   jax.ShapeDtypeStruct((b, s, N_HEADS), F32),
                   jax.ShapeDtypeStruct((b, CONV_W - 1, CONV_DIM), F32)],
        scratch_shapes=[
            pltpu.VMEM((N_PAIRS, nt, LANES, t), BF16),
            pltpu.VMEM((N_PAIRS, nt, t, 2 * LANES), BF16),
            pltpu.VMEM((nt, N_HEADS, t), F32),
            pltpu.VMEM((N_HEADS, t, LANES), BF16),
            pltpu.VMEM((N_HEADS, t, 1), F32),
            pltpu.VMEM((N_HEADS, t, 2 * LANES), F32),
            pltpu.VMEM((1, F_PAD), F32),
            pltpu.VMEM((SUBLANES, CONV_DIM), F32),
        ],
        compiler_params=pltpu.CompilerParams(
            dimension_semantics=("arbitrary", "arbitrary"), vmem_limit_bytes=VMEM_LIMIT),
        name="prompt_layer",
    )(x, *consts)


def _sample_kernel(x_ref, pk_ref, pv_ref, pf_ref, st_ref,
                   ng_ref, w_ref, bf_ref, qg_ref, kg_ref, cw_ref, ag_ref, cg_ref, wo_ref,
                   bd_ref, tri_ref, upper_ref, ones_ref, blk_ref,
                   y_ref, ko_ref, vo_ref, fo_ref, co_ref):
    n = x_ref.shape[1]
    n_blk = pf_ref.shape[1] // N_HEADS
    x = x_ref[0]
    h, q, k, v, logf = _front(x, ng_ref, w_ref, bf_ref, qg_ref, kg_ref, bd_ref)
    ko_ref[0] = k
    vo_ref[0] = v
    fo_ref[0] = logf[:, :N_HEADS]

    pf = pf_ref[0]
    tot = _dot_exact_lhs(pf, ones_ref[...])
    before = _dot_exact_rhs(blk_ref[...], tot)
    c_past = _dot_exact_lhs(pf, upper_ref[...]) + before
    ck_past = jnp.concatenate(
        [c_past[bi * N_HEADS:(bi + 1) * N_HEADS] for bi in range(n_blk)], axis=1)
    past_total = (before + tot)[(n_blk - 1) * N_HEADS:, :LANES]

    zpad = jnp.zeros((LANES - n, F_PAD), F32)
    c_new = _dot_exact_rhs(tri_ref[...], jnp.concatenate([logf, zpad], axis=0))
    ck_new = c_new.T[:N_HEADS, :] + past_total

    row = lax.broadcasted_iota(jnp.int32, (n, LANES), 0)
    col = lax.broadcasted_iota(jnp.int32, (n, LANES), 1)
    keep = col <= row
    qms = _masked_queries(q)
    outs = []
    for hd in range(N_HEADS):
        p = hd // 2
        if hd % 2 == 0:
            sl = slice(p * LANES, (p + 1) * LANES)
            kp = pk_ref[0, :, sl].astype(BF16)
            vp = pv_ref[0, :, sl].astype(BF16)
            vpx = jnp.concatenate([vp, jnp.ones_like(vp)], axis=1)
            kn = jnp.concatenate([k[:, sl], jnp.zeros((LANES - n, LANES), F32)], axis=0).astype(BF16)
            vn = jnp.concatenate([v[:, sl], jnp.zeros((LANES - n, LANES), F32)], axis=0).astype(BF16)
            vnx = jnp.concatenate([vn, jnp.ones_like(vn)], axis=1)
        z_past = _dot_nt(qms[hd], kp) - ck_past[hd:hd + 1, :]
        z_new = jnp.where(keep, _dot_nt(qms[hd], kn) - ck_new[hd:hd + 1, :], NEG)
        m = jnp.maximum(jnp.max(z_past, axis=-1, keepdims=True), jnp.max(z_new, axis=-1, keepdims=True))
        outs.append(_dot(jnp.exp(z_past - m).astype(BF16), vpx) + _dot(jnp.exp(z_new - m).astype(BF16), vnx))

    att = _merge_heads(outs)
    st = st_ref[0]
    u_prev8 = jnp.concatenate([jnp.zeros((SUBLANES - (CONV_W - 1), CONV_DIM), F32), st], axis=0)
    y, u = _back(x, h, att, u_prev8, w_ref, cw_ref, ag_ref, cg_ref, wo_ref)
    y_ref[0] = y
    co_ref[0] = u[n - (CONV_W - 1):, :]


def _sample_layer(x, pk, pv, pf, st, ng, w, bfp, qg, kg, cw, ag, cg, wo, bd, tri, upper, ones, blk):
    b, n, d = x.shape
    past = pk.shape[1]
    consts = (ng, w, bfp, qg, kg, cw, ag, cg, wo, bd, tri, upper, ones, blk)
    per_b = lambda a: pl.BlockSpec((1,) + a.shape[1:], lambda bi: (bi,) + (0,) * (a.ndim - 1))
    out_shape = [jax.ShapeDtypeStruct((b, n, d), F32),
                 jax.ShapeDtypeStruct((b, n, ATT_DIM), F32),
                 jax.ShapeDtypeStruct((b, n, ATT_DIM), F32),
                 jax.ShapeDtypeStruct((b, n, N_HEADS), F32),
                 jax.ShapeDtypeStruct((b, CONV_W - 1, CONV_DIM), F32)]
    return pl.pallas_call(
        _sample_kernel,
        grid=(b,),
        in_specs=[per_b(a) for a in (x, pk, pv, pf, st)] + [_const_spec(a.shape) for a in consts],
        out_specs=[per_b(a) for a in out_shape],
        out_shape=out_shape,
        compiler_params=pltpu.CompilerParams(
            dimension_semantics=("arbitrary",), vmem_limit_bytes=VMEM_LIMIT),
        name="sample_layer",
    )(x, pk, pv, pf, st, *consts)


def _tri_lower(n):
    return jnp.asarray(np.tril(np.ones((n, n), np.float32)), BF16)


def kernel(x_prompt, x_sample, cache_k, cache_v, cache_logf, state_conv, norm_g, w_in, b_f,
           q_norm_g, k_norm_g, conv_w, att_out_g, conv_out_g, w_out):
    depth = w_in.shape[0]
    bp, sp, _ = x_prompt.shape
    bs, ns, _ = x_sample.shape
    past = cache_k.shape[2]
    n_blk = past // PAST_BLK

    bd = jnp.asarray(np.kron(np.eye(N_HEADS, dtype=np.float32), np.ones((HEAD_DIM, HEAD_DIM), np.float32)), BF16)
    tri_t = _tri_lower(SEQ_TILE)
    tri_l = _tri_lower(LANES)
    upper = jnp.asarray(np.triu(np.ones((PAST_BLK, PAST_BLK), np.float32)), BF16)
    ones = jnp.ones((PAST_BLK, PAST_BLK), BF16)
    r = np.arange(n_blk * N_HEADS)
    blk = jnp.asarray(((r[:, None] % N_HEADS == r[None, :] % N_HEADS)
                       & (r[None, :] // N_HEADS < r[:, None] // N_HEADS)).astype(np.float32), BF16)

    f0 = 4 * ATT_DIM
    yp, ys = x_prompt, x_sample
    outs = [[] for _ in range(8)]
    for l in range(depth):
        w = jnp.concatenate([w_in[l][:, :f0], w_in[l][:, f0 + N_HEADS:], w_in[l][:, f0:f0 + N_HEADS],
                             jnp.zeros((D_MODEL, F_PAD - N_HEADS), F32)], axis=1).astype(BF16)
        wo = w_out[l].astype(BF16)
        ng = norm_g[l][None, :]
        bfp = jnp.concatenate([b_f[l], jnp.zeros((F_PAD - N_HEADS,), F32)])[None, :]
        qg = jnp.tile(q_norm_g[l], N_HEADS)[None, :]
        kg = jnp.tile(k_norm_g[l], N_HEADS)[None, :]
        ag = att_out_g[l][None, :]
        cg = conv_out_g[l][None, :]
        cw = conv_w[l]

        yp, k1, v1, f1, c1 = _prompt_layer(yp, ng, w, bfp, qg, kg, cw, ag, cg, wo, bd, tri_t)

        pk = cache_k[l].reshape(bs, past, ATT_DIM)
        pv = cache_v[l].reshape(bs, past, ATT_DIM)
        pf = cache_logf[l].reshape(bs, n_blk, PAST_BLK, N_HEADS).transpose(0, 1, 3, 2)
        pf = pf.reshape(bs, n_blk * N_HEADS, PAST_BLK)
        ys, k2, v2, f2, c2 = _sample_layer(ys, pk, pv, pf, state_conv[l], ng, w, bfp, qg, kg, cw, ag, cg, wo,
                                           bd, tri_l, upper, ones, blk)
        for lst, a in zip(outs, (k1.reshape(bp, sp, N_HEADS, HEAD_DIM), v1.reshape(bp, sp, N_HEADS, HEAD_DIM),
                                 f1, c1,
                                 k2.reshape(bs, ns, N_HEADS, HEAD_DIM), v2.reshape(bs, ns, N_HEADS, HEAD_DIM),
                                 f2, c2)):
            lst.append(a)
    return (yp, ys) + tuple(jnp.stack(o) for o in outs)
```

```python
import functools

import numpy as np
import jax
import jax.numpy as jnp
from jax import lax
from jax.experimental import pallas as pl
from jax.experimental.pallas import tpu as pltpu

D_MODEL = 1024
HEAD_DIM = 64
N_HEADS = 8
N_PAIRS = N_HEADS // 2
ATT_DIM = N_HEADS * HEAD_DIM
CONV_DIM = 512
CONV_W = 3
EPS = 1e-6
NEG = -1e30
ATTN_SCALE = HEAD_DIM ** -0.5
LOG2E = 1.4426950408889634

LANES = 128
SUBLANES = 8
BF16_ROWS = 16
F_PAD = LANES
F_ROWS = BF16_ROWS
OFF_Q, OFF_K, OFF_V, OFF_GA = 0, 512, 1024, 1536
OFF_CB, OFF_CC, OFF_CH, OFF_GC = 2048, 2560, 3072, 3584
OFF_F = 4096

SEQ_TILE = 256
PV_ROWS = HEAD_DIM + BF16_ROWS
PAST_BLK = 256
VMEM_LIMIT = 58 * 1024 * 1024

BF16 = jnp.bfloat16
F32 = jnp.float32


def _dot(a, b):
    return jnp.dot(a, b, preferred_element_type=F32)


def _dot_nt(a, b):
    return lax.dot_general(a, b, (((1,), (1,)), ((), ())), preferred_element_type=F32)


def _split3(x):
    hi = x.astype(BF16)
    r = x - hi.astype(F32)
    mid = r.astype(BF16)
    lo = (r - mid.astype(F32)).astype(BF16)
    return hi, mid, lo


def _dot_exact_rhs(m01, x):
    hi, mid, lo = _split3(x)
    return _dot(m01, hi) + _dot(m01, mid) + _dot(m01, lo)


def _dot_exact_lhs(x, m01):
    hi, mid, lo = _split3(x)
    return _dot(hi, m01) + _dot(mid, m01) + _dot(lo, m01)


def _silu(x):
    return x / (1.0 + jnp.exp(-x))


def _log_sigmoid(x):
    return jnp.minimum(x, 0.0) - jnp.log1p(jnp.exp(-jnp.abs(x)))


def _lanes(rep, t):
    return jnp.concatenate([rep] * (t // LANES), axis=-1)


def _rms_t(xt, g_rep):
    t = xt.shape[1]
    ms = jnp.mean(xt * xt, axis=0, keepdims=True)
    return xt * lax.rsqrt(ms + EPS) * _lanes(g_rep, t)


def _head_rms_t(xt, g_rep):
    t = xt.shape[1]
    x3 = xt.reshape(N_HEADS, HEAD_DIM, t)
    ms = jnp.mean(x3 * x3, axis=1, keepdims=True)
    return (x3 * lax.rsqrt(ms + EPS) * _lanes(g_rep, t)[None]).reshape(ATT_DIM, t)


def _shift_lanes(ut, prev, k):
    r = pltpu.roll(ut, k, axis=1)
    pr = pltpu.roll(prev, k, axis=1)
    lane = lax.broadcasted_iota(jnp.int32, prev.shape, 1)
    head = jnp.where(lane < k, pr, r[:, :LANES])
    return jnp.concatenate([head, r[:, LANES:]], axis=1)


def _prompt_kernel(n_aliased, x_ref, ng_ref, wt_ref, wo_ref, bf_ref, qg_ref, kg_ref, cw_ref, ag_ref, cg_ref,
                   up_ref, *refs):
    (y_ref, ko_ref, vo_ref, fo_ref, co_ref,
     kn_ref, vt_ref, ck_ref, qm_ref, z_ref, m_ref, al_ref, acc_ref, carry_ref, uprev_ref,
     gate_ref) = refs[n_aliased:]
    i = pl.program_id(1)
    nt = pl.num_programs(1)
    t = SEQ_TILE

    @pl.when(i == 0)
    def _():
        carry_ref[...] = jnp.zeros_like(carry_ref)
        uprev_ref[...] = jnp.zeros_like(uprev_ref)

    x = x_ref[0]
    ms = jnp.mean(x * x, axis=-1, keepdims=True)
    ht = (x * lax.rsqrt(ms + EPS) * ng_ref[...]).astype(BF16).T

    def proj(off, n):
        return _dot(wt_ref[off:off + n, :], ht)

    kt = _head_rms_t(proj(OFF_K, ATT_DIM), kg_ref[...])
    ko_ref[0, 0] = kt.reshape(N_HEADS, HEAD_DIM, t)
    kn_ref[i] = kt.astype(BF16).T
    vt = proj(OFF_V, ATT_DIM)
    vo_ref[0, 0] = vt.reshape(N_HEADS, HEAD_DIM, t)
    ones_rows = jnp.ones((PV_ROWS - HEAD_DIM, t), BF16)
    for hd in range(N_HEADS):
        vt_ref[hd, i] = jnp.concatenate([vt[hd * HEAD_DIM:(hd + 1) * HEAD_DIM].astype(BF16), ones_rows], axis=0)

    qt = (_head_rms_t(proj(OFF_Q, ATT_DIM), qg_ref[...]) * (ATTN_SCALE * LOG2E)).astype(BF16)
    zero_rows = jnp.zeros((HEAD_DIM, t), BF16)
    for hd in range(N_HEADS):
        q_h = qt[hd * HEAD_DIM:(hd + 1) * HEAD_DIM]
        qm_ref[hd] = jnp.concatenate([q_h, zero_rows] if hd % 2 == 0 else [zero_rows, q_h], axis=0)

    logf = _log_sigmoid(proj(OFF_F, F_ROWS)[:N_HEADS] + _lanes(bf_ref[...], t))
    fo_ref[0, 0] = logf
    c = _dot_exact_lhs(logf, up_ref[...]) + _lanes(carry_ref[...], t)
    carry_ref[...] = jnp.broadcast_to(c[:, t - 1:], carry_ref.shape)
    c_pos = jnp.concatenate([c, jnp.zeros((LANES - N_HEADS, t), F32)], axis=0).T
    for hd in range(N_HEADS):
        ck_ref[hd, i] = jnp.broadcast_to(c_pos[:, hd:hd + 1] * LOG2E, (t, LANES))

    m_ref[...] = jnp.full(m_ref.shape, NEG, F32)
    acc_ref[...] = jnp.zeros_like(acc_ref)

    def scores(hd, j, masked):
        p = hd // 2
        z = _dot(kn_ref[j, :, p * LANES:(p + 1) * LANES], qm_ref[hd]) - _lanes(ck_ref[hd, j], t)
        if masked:
            keep = (lax.broadcasted_iota(jnp.int32, (t, t), 0) <= lax.broadcasted_iota(jnp.int32, (t, t), 1))
            z = jnp.where(keep, z, NEG)
        z_ref[hd] = z
        m_old = m_ref[hd]
        m_new = jnp.maximum(m_old, jnp.max(z, axis=0, keepdims=True))
        al_ref[hd] = jnp.exp2(m_old - m_new)
        m_ref[hd] = m_new

    def accumulate(hd, j):
        pr = jnp.exp2(z_ref[hd] - m_ref[hd]).astype(BF16)
        acc_ref[hd] = al_ref[hd] * acc_ref[hd] + _dot(vt_ref[hd, j], pr)

    for hd in range(N_HEADS):
        scores(hd, i, True)

    ut = proj(OFF_CC, CONV_DIM) * proj(OFF_CH, CONV_DIM)
    prev = uprev_ref[...]
    yc = (_lanes(cw_ref[0], t) * _shift_lanes(ut, prev, 2) + _lanes(cw_ref[1], t) * _shift_lanes(ut, prev, 1)
          + _lanes(cw_ref[2], t) * ut)
    zn = _rms_t(proj(OFF_CB, CONV_DIM) * yc, cg_ref[...]) * _silu(proj(OFF_GC, CONV_DIM))
    uprev_ref[...] = ut[:, t - LANES:]
    y_ref[0] = x + _dot(zn.astype(BF16).T, wo_ref[ATT_DIM:, :])
    gate_ref[...] = _silu(proj(OFF_GA, ATT_DIM))

    def body(j, carry):
        prev = jnp.where(j == 0, i, j - 1)
        for hd in range(N_HEADS):
            accumulate(hd, prev)
            scores(hd, j, False)
        return carry

    lax.fori_loop(0, i, body, 0)
    last = jnp.maximum(i - 1, 0)
    for hd in range(N_HEADS):
        accumulate(hd, last)

    outs = []
    for hd in range(N_HEADS):
        a = acc_ref[hd]
        outs.append(a[:HEAD_DIM] / a[HEAD_DIM:HEAD_DIM + 1])
    att = jnp.concatenate(outs, axis=0)
    an = (_rms_t(att, ag_ref[...]) * gate_ref[...]).astype(BF16).T
    y_ref[0] = y_ref[0] + _dot(an, wo_ref[:ATT_DIM, :])

    @pl.when(i == nt - 1)
    def _():
        co_ref[0] = uprev_ref[...].T[LANES - (CONV_W - 1):, :]


def _const_spec(shape):
    nd = len(shape)
    return pl.BlockSpec(shape, lambda *_: (0,) * nd)


def _prompt_layer(layer, depth, x, consts, stacked):
    b, s, d = x.shape
    t = SEQ_TILE
    nt = s // t
    any_spec = pl.BlockSpec(memory_space=pl.ANY)
    n_in = 1 + len(consts)
    return pl.pallas_call(
        functools.partial(_prompt_kernel, len(stacked)),
        grid=(b, nt),
        in_specs=[pl.BlockSpec((1, t, d), lambda bi, ti: (bi, ti, 0))]
                 + [_const_spec(a.shape) for a in consts] + [any_spec] * len(stacked),
        out_specs=[pl.BlockSpec((1, t, d), lambda bi, ti: (bi, ti, 0)),
                   pl.BlockSpec((1, 1, N_HEADS, HEAD_DIM, t), lambda bi, ti: (layer, bi, 0, 0, ti)),
                   pl.BlockSpec((1, 1, N_HEADS, HEAD_DIM, t), lambda bi, ti: (layer, bi, 0, 0, ti)),
                   pl.BlockSpec((1, 1, N_HEADS, t), lambda bi, ti: (layer, bi, 0, ti)),
                   pl.BlockSpec((1, CONV_W - 1, CONV_DIM), lambda bi, ti: (bi, 0, 0))],
        out_shape=[jax.ShapeDtypeStruct((b, s, d), F32),
                   jax.ShapeDtypeStruct((depth, b, N_HEADS, HEAD_DIM, s), F32),
                   jax.ShapeDtypeStruct((depth, b, N_HEADS, HEAD_DIM, s), F32),
                   jax.ShapeDtypeStruct((depth, b, N_HEADS, s), F32),
                   jax.ShapeDtypeStruct((b, CONV_W - 1, CONV_DIM), F32)],
        input_output_aliases={n_in + a: 1 + a for a in range(len(stacked))},
        scratch_shapes=[
            pltpu.VMEM((nt, t, ATT_DIM), BF16),
            pltpu.VMEM((N_HEADS, nt, PV_ROWS, t), BF16),
            pltpu.VMEM((N_HEADS, nt, t, LANES), F32),
            pltpu.VMEM((N_HEADS, LANES, t), BF16),
            pltpu.VMEM((N_HEADS, t, t), F32),
            pltpu.VMEM((N_HEADS, 1, t), F32),
            pltpu.VMEM((N_HEADS, 1, t), F32),
            pltpu.VMEM((N_HEADS, PV_ROWS, t), F32),
            pltpu.VMEM((N_HEADS, LANES), F32),
            pltpu.VMEM((CONV_DIM, LANES), F32),
            pltpu.VMEM((ATT_DIM, t), F32),
        ],
        compiler_params=pltpu.CompilerParams(
            dimension_semantics=("arbitrary", "arbitrary"), vmem_limit_bytes=VMEM_LIMIT),
        name="prompt_layer",
    )(x, *consts, *stacked)


def _rms(x, g):
    ms = jnp.mean(x * x, axis=-1, keepdims=True)
    return x * lax.rsqrt(ms + EPS) * g


def _head_rms(x, g_tiled, bd):
    ss = _dot((x * x).astype(BF16), bd)
    return x * lax.rsqrt(ss * (1.0 / HEAD_DIM) + EPS) * g_tiled


def _shift_rows(u, prev8, k):
    r = pltpu.roll(u, k, axis=0)
    pr = pltpu.roll(prev8, k, axis=0)
    rowid = lax.broadcasted_iota(jnp.int32, prev8.shape, 0)
    head = jnp.where(rowid < k, pr, r[0:SUBLANES])
    return jnp.concatenate([head, r[SUBLANES:]], axis=0)


def _sample_kernel(x_ref, pk_ref, pv_ref, pf_ref, st_ref,
                   ng_ref, w_ref, bf_ref, qg_ref, kg_ref, cw_ref, ag_ref, cg_ref, wo_ref,
                   bd_ref, tri_ref, upper_ref, ones_ref, blk_ref,
                   y_ref, ko_ref, vo_ref, fo_ref, co_ref):
    n = x_ref.shape[1]
    n_blk = pf_ref.shape[1] // N_HEADS
    past = pk_ref.shape[-1]
    x = x_ref[0]
    h = _rms(x, ng_ref[...]).astype(BF16)
    bd = bd_ref[...]
    q = _head_rms(_dot(h, w_ref[:, OFF_Q:OFF_Q + ATT_DIM]), qg_ref[...], bd) * (ATTN_SCALE * LOG2E)
    k = _head_rms(_dot(h, w_ref[:, OFF_K:OFF_K + ATT_DIM]), kg_ref[...], bd)
    v = _dot(h, w_ref[:, OFF_V:OFF_V + ATT_DIM])
    logf = _log_sigmoid(_dot(h, w_ref[:, OFF_F:OFF_F + F_PAD]) + bf_ref[...])
    ko_ref[0] = k
    vo_ref[0] = v
    fo_ref[0] = logf[:, :N_HEADS]

    pf = pf_ref[0]
    tot = _dot_exact_lhs(pf, ones_ref[...])
    before = _dot_exact_rhs(blk_ref[...], tot)
    c_past = _dot_exact_lhs(pf, upper_ref[...]) + before
    ck_past = jnp.concatenate(
        [c_past[bi * N_HEADS:(bi + 1) * N_HEADS] for bi in range(n_blk)], axis=1) * LOG2E
    past_total = (before + tot)[(n_blk - 1) * N_HEADS:, :LANES]

    zpad = jnp.zeros((LANES - n, F_PAD), F32)
    c_new = _dot_exact_rhs(tri_ref[...], jnp.concatenate([logf, zpad], axis=0))
    ck_new = (c_new.T[:N_HEADS, :] + past_total) * LOG2E

    row = lax.broadcasted_iota(jnp.int32, (n, LANES), 0)
    col = lax.broadcasted_iota(jnp.int32, (n, LANES), 1)
    keep = col <= row
    low = lax.broadcasted_iota(jnp.int32, (n, LANES), 1) < HEAD_DIM
    cols = []
    for p in range(N_PAIRS):
        sl = slice(p * LANES, (p + 1) * LANES)
        kpt = pk_ref[0, 0, 2 * p:2 * p + 2].reshape(LANES, past).astype(BF16)
        vpt = pv_ref[0, 0, 2 * p:2 * p + 2].reshape(LANES, past).astype(BF16)
        zrows = jnp.zeros((LANES - n, LANES), F32)
        kn = jnp.concatenate([k[:, sl], zrows], axis=0).astype(BF16)
        vn = jnp.concatenate([v[:, sl], zrows], axis=0).astype(BF16)
        o = []
        for hd in (2 * p, 2 * p + 1):
            keep_q = low if hd % 2 == 0 else jnp.logical_not(low)
            qm = jnp.where(keep_q, q[:, sl], 0.0).astype(BF16)
            z_past = _dot(qm, kpt) - ck_past[hd:hd + 1, :]
            z_new = jnp.where(keep, _dot_nt(qm, kn) - ck_new[hd:hd + 1, :], NEG)
            m = jnp.maximum(jnp.max(z_past, axis=-1, keepdims=True), jnp.max(z_new, axis=-1, keepdims=True))
            p_past = jnp.exp2(z_past - m)
            p_new = jnp.exp2(z_new - m)
            l = jnp.sum(p_past, axis=-1, keepdims=True) + jnp.sum(p_new, axis=-1, keepdims=True)
            o.append((_dot_nt(p_past.astype(BF16), vpt) + _dot(p_new.astype(BF16), vn)) / l)
        cols.append(jnp.where(low, o[0], o[1]))
    att = jnp.concatenate(cols, axis=1)

    st = st_ref[0, 0]
    u_prev8 = jnp.concatenate([jnp.zeros((SUBLANES - (CONV_W - 1), CONV_DIM), F32), st], axis=0)
    cc = _dot(h, w_ref[:, OFF_CC:OFF_CC + CONV_DIM])
    ch = _dot(h, w_ref[:, OFF_CH:OFF_CH + CONV_DIM])
    u = cc * ch
    cw = cw_ref[...]
    yc = cw[0:1] * _shift_rows(u, u_prev8, 2) + cw[1:2] * _shift_rows(u, u_prev8, 1) + cw[2:3] * u
    z = _dot(h, w_ref[:, OFF_CB:OFF_CB + CONV_DIM]) * yc
    zn = _rms(z, cg_ref[...]) * _silu(_dot(h, w_ref[:, OFF_GC:OFF_GC + CONV_DIM]))
    an = _rms(att, ag_ref[...]) * _silu(_dot(h, w_ref[:, OFF_GA:OFF_GA + ATT_DIM]))
    mix = jnp.concatenate([an, zn], axis=1).astype(BF16)
    y_ref[0] = x + _dot(mix, wo_ref[...])
    co_ref[0] = u[n - (CONV_W - 1):, :]


def _sample_layer(layer, x, pk_t, pv_t, pf, st_all, consts):
    b, n, d = x.shape
    past = pk_t.shape[-1]
    per_b = lambda a: pl.BlockSpec((1,) + a.shape[1:], lambda bi: (bi,) + (0,) * (a.ndim - 1))
    per_lb = lambda a: pl.BlockSpec((1, 1) + a.shape[2:], lambda bi: (layer, bi) + (0,) * (a.ndim - 2))
    out_shape = [jax.ShapeDtypeStruct((b, n, d), F32),
                 jax.ShapeDtypeStruct((b, n, ATT_DIM), F32),
                 jax.ShapeDtypeStruct((b, n, ATT_DIM), F32),
                 jax.ShapeDtypeStruct((b, n, N_HEADS), F32),
                 jax.ShapeDtypeStruct((b, CONV_W - 1, CONV_DIM), F32)]
    return pl.pallas_call(
        _sample_kernel,
        grid=(b,),
        in_specs=[per_b(x), per_lb(pk_t), per_lb(pv_t), per_b(pf), per_lb(st_all)]
                 + [_const_spec(a.shape) for a in consts],
        out_specs=[per_b(a) for a in out_shape],
        out_shape=out_shape,
        compiler_params=pltpu.CompilerParams(
            dimension_semantics=("arbitrary",), vmem_limit_bytes=VMEM_LIMIT),
        name="sample_layer",
    )(x, pk_t, pv_t, pf, st_all, *consts)


def _rep(v):
    return jnp.broadcast_to(v[..., None], v.shape + (LANES,))


def kernel(x_prompt, x_sample, cache_k, cache_v, cache_logf, state_conv, norm_g, w_in, b_f,
           q_norm_g, k_norm_g, conv_w, att_out_g, conv_out_g, w_out):
    depth = w_in.shape[0]
    bp, sp, _ = x_prompt.shape
    bs, ns, _ = x_sample.shape
    past = cache_k.shape[2]
    n_blk = past // PAST_BLK

    bd = jnp.asarray(np.kron(np.eye(N_HEADS, dtype=np.float32), np.ones((HEAD_DIM, HEAD_DIM), np.float32)), BF16)
    up_t = jnp.asarray(np.triu(np.ones((SEQ_TILE, SEQ_TILE), np.float32)), BF16)
    tri_l = jnp.asarray(np.tril(np.ones((LANES, LANES), np.float32)), BF16)
    upper = jnp.asarray(np.triu(np.ones((PAST_BLK, PAST_BLK), np.float32)), BF16)
    ones = jnp.ones((PAST_BLK, PAST_BLK), BF16)
    r = np.arange(n_blk * N_HEADS)
    blk = jnp.asarray(((r[:, None] % N_HEADS == r[None, :] % N_HEADS)
                       & (r[None, :] // N_HEADS < r[:, None] // N_HEADS)).astype(np.float32), BF16)

    pk_t = jnp.transpose(cache_k, (0, 1, 3, 4, 2))
    pv_t = jnp.transpose(cache_v, (0, 1, 3, 4, 2))

    f0 = 4 * ATT_DIM
    stacked = ()
    yp, ys = x_prompt, x_sample
    outs = [[] for _ in range(5)]
    for l in range(depth):
        w_l = w_in[l]
        w_nat = jnp.concatenate([w_l[:, :f0], w_l[:, f0 + N_HEADS:], w_l[:, f0:f0 + N_HEADS],
                                 jnp.zeros((D_MODEL, F_PAD - N_HEADS), F32)], axis=1).astype(BF16)
        w_t = jnp.transpose(w_l)
        w_t = jnp.concatenate([w_t[:f0], w_t[f0 + N_HEADS:], w_t[f0:f0 + N_HEADS],
                               jnp.zeros((F_ROWS - N_HEADS, D_MODEL), F32)], axis=0).astype(BF16)
        wo = w_out[l].astype(BF16)
        ng = norm_g[l][None, :]

        p_consts = (ng, w_t, wo, _rep(b_f[l]), _rep(q_norm_g[l]), _rep(k_norm_g[l]), _rep(conv_w[l]),
                    _rep(att_out_g[l]), _rep(conv_out_g[l]), up_t)
        yp, *stacked, c1 = _prompt_layer(l, depth, yp, p_consts, tuple(stacked))

        bfp = jnp.concatenate([b_f[l], jnp.zeros((F_PAD - N_HEADS,), F32)])[None, :]
        s_consts = (ng, w_nat, bfp, jnp.tile(q_norm_g[l], N_HEADS)[None, :], jnp.tile(k_norm_g[l], N_HEADS)[None, :],
                    conv_w[l], att_out_g[l][None, :], conv_out_g[l][None, :], wo,
                    bd, tri_l, upper, ones, blk)
        pf = cache_logf[l].reshape(bs, n_blk, PAST_BLK, N_HEADS).transpose(0, 1, 3, 2)
        pf = pf.reshape(bs, n_blk * N_HEADS, PAST_BLK)
        ys, k2, v2, f2, c2 = _sample_layer(l, ys, pk_t, pv_t, pf, state_conv, s_consts)
        for lst, a in zip(outs, (c1, k2.reshape(bs, ns, N_HEADS, HEAD_DIM), v2.reshape(bs, ns, N_HEADS, HEAD_DIM),
                                 f2, c2)):
            lst.append(a)
    k_all, v_all, f_all = stacked
    k_prompt = jnp.transpose(k_all, (0, 1, 4, 2, 3))
    v_prompt = jnp.transpose(v_all, (0, 1, 4, 2, 3))
    f_prompt = jnp.transpose(f_all, (0, 1, 3, 2))
    c_prompt, k_s, v_s, f_s, c_s = (jnp.stack(o) for o in outs)
    return (yp, ys, k_prompt, v_prompt, f_prompt, c_prompt, k_s, v_s, f_s, c_s)
```

```python
import functools

import numpy as np
import jax
import jax.numpy as jnp
from jax import lax
from jax.experimental import pallas as pl
from jax.experimental.pallas import tpu as pltpu

D_MODEL = 1024
HEAD_DIM = 64
N_HEADS = 8
N_PAIRS = N_HEADS // 2
ATT_DIM = N_HEADS * HEAD_DIM
CONV_DIM = 512
CONV_W = 3
EPS = 1e-6
NEG = -1e30
ATTN_SCALE = HEAD_DIM ** -0.5
LOG2E = 1.4426950408889634

LANES = 128
SUBLANES = 8
BF16_ROWS = 16
F_PAD = LANES
F_ROWS = BF16_ROWS
OFF_Q, OFF_K, OFF_V, OFF_GA = 0, 512, 1024, 1536
OFF_CB, OFF_CC, OFF_CH, OFF_GC = 2048, 2560, 3072, 3584
OFF_F = 4096

SEQ_TILE = 512
KEY_TILE = 256
PV_ROWS = HEAD_DIM + BF16_ROWS
PAST_BLK = 256
VMEM_LIMIT = 58 * 1024 * 1024

BF16 = jnp.bfloat16
F32 = jnp.float32


def _dot(a, b):
    return jnp.dot(a, b, preferred_element_type=F32)


def _dot_nt(a, b):
    return lax.dot_general(a, b, (((1,), (1,)), ((), ())), preferred_element_type=F32)


def _split3(x):
    hi = x.astype(BF16)
    r = x - hi.astype(F32)
    mid = r.astype(BF16)
    lo = (r - mid.astype(F32)).astype(BF16)
    return hi, mid, lo


def _dot_exact_rhs(m01, x):
    hi, mid, lo = _split3(x)
    return _dot(m01, hi) + _dot(m01, mid) + _dot(m01, lo)


def _dot_exact_lhs(x, m01):
    hi, mid, lo = _split3(x)
    return _dot(hi, m01) + _dot(mid, m01) + _dot(lo, m01)


def _silu(x):
    return x / (1.0 + jnp.exp(-x))


def _log_sigmoid(x):
    return jnp.minimum(x, 0.0) - jnp.log1p(jnp.exp(-jnp.abs(x)))


def _lanes(rep, t):
    return jnp.concatenate([rep] * (t // LANES), axis=-1)


def _rms_t(xt, g_rep):
    t = xt.shape[1]
    ms = jnp.mean(xt * xt, axis=0, keepdims=True)
    return xt * lax.rsqrt(ms + EPS) * _lanes(g_rep, t)


def _head_rms_t(xt, g_rep):
    t = xt.shape[1]
    x3 = xt.reshape(N_HEADS, HEAD_DIM, t)
    ms = jnp.mean(x3 * x3, axis=1, keepdims=True)
    return (x3 * lax.rsqrt(ms + EPS) * _lanes(g_rep, t)[None]).reshape(ATT_DIM, t)


def _shift_lanes(ut, prev, k):
    r = pltpu.roll(ut, k, axis=1)
    pr = pltpu.roll(prev, k, axis=1)
    lane = lax.broadcasted_iota(jnp.int32, prev.shape, 1)
    head = jnp.where(lane < k, pr, r[:, :LANES])
    return jnp.concatenate([head, r[:, LANES:]], axis=1)


def _prompt_kernel(n_aliased, x_ref, ng_ref, wt_ref, wo_ref, bf_ref, qg_ref, kg_ref, cw_ref, ag_ref, cg_ref,
                   up_ref, *refs):
    (y_ref, ko_ref, vo_ref, fo_ref, co_ref,
     kn_ref, vt_ref, ck_ref, qm_ref, z_ref, m_ref, al_ref, acc_ref, carry_ref, uprev_ref,
     gate_ref, zt_ref) = refs[n_aliased:]
    i = pl.program_id(1)
    nt = pl.num_programs(1)
    t = SEQ_TILE
    tk = KEY_TILE
    n_sub = t // tk
    base = n_sub * i

    @pl.when(i == 0)
    def _():
        carry_ref[...] = jnp.zeros_like(carry_ref)
        uprev_ref[...] = jnp.zeros_like(uprev_ref)

    x = x_ref[0]
    ms = jnp.mean(x * x, axis=-1, keepdims=True)
    ht = (x * lax.rsqrt(ms + EPS) * ng_ref[...]).astype(BF16).T

    def proj(off, n):
        half = n // 2
        if half % BF16_ROWS:
            return _dot(wt_ref[off:off + n, :], ht)
        return jnp.concatenate([_dot(wt_ref[off:off + half, :], ht),
                                _dot(wt_ref[off + half:off + n, :], ht)], axis=0)

    def out_proj(a, row0):
        half = D_MODEL // 2
        return jnp.concatenate([_dot(a, wo_ref[row0:row0 + ATT_DIM, :half]),
                                _dot(a, wo_ref[row0:row0 + ATT_DIM, half:])], axis=1)

    logf = _log_sigmoid(proj(OFF_F, F_ROWS)[:N_HEADS] + _lanes(bf_ref[...], t))
    fo_ref[0, 0] = logf

    kt = _head_rms_t(proj(OFF_K, ATT_DIM), kg_ref[...])
    ko_ref[0, 0] = kt.reshape(N_HEADS, HEAD_DIM, t)
    kn = kt.astype(BF16).T
    for s in range(n_sub):
        kn_ref[base + s] = kn[s * tk:(s + 1) * tk]
    vt = proj(OFF_V, ATT_DIM)
    vo_ref[0, 0] = vt.reshape(N_HEADS, HEAD_DIM, t)
    ones_rows = jnp.ones((PV_ROWS - HEAD_DIM, tk), BF16)
    for hd in range(N_HEADS):
        v_h = vt[hd * HEAD_DIM:(hd + 1) * HEAD_DIM].astype(BF16)
        for s in range(n_sub):
            vt_ref[hd, base + s] = jnp.concatenate([v_h[:, s * tk:(s + 1) * tk], ones_rows], axis=0)

    qt = (_head_rms_t(proj(OFF_Q, ATT_DIM), qg_ref[...]) * (ATTN_SCALE * LOG2E)).astype(BF16)
    zero_rows = jnp.zeros((HEAD_DIM, t), BF16)
    for hd in range(N_HEADS):
        q_h = qt[hd * HEAD_DIM:(hd + 1) * HEAD_DIM]
        qm_ref[hd] = jnp.concatenate([q_h, zero_rows] if hd % 2 == 0 else [zero_rows, q_h], axis=0)

    c = _dot_exact_lhs(logf, up_ref[...]) + _lanes(carry_ref[...], t)
    carry_ref[...] = jnp.broadcast_to(c[:, t - 1:], carry_ref.shape)
    c_pos = jnp.concatenate([c, jnp.zeros((LANES - N_HEADS, t), F32)], axis=0).T
    for hd in range(N_HEADS):
        c_rep = jnp.broadcast_to(c_pos[:, hd:hd + 1] * LOG2E, (t, LANES))
        for s in range(n_sub):
            ck_ref[hd, base + s] = c_rep[s * tk:(s + 1) * tk]

    m_ref[...] = jnp.full(m_ref.shape, NEG, F32)
    acc_ref[...] = jnp.zeros_like(acc_ref)

    def track_max(hd, z):
        z_ref[hd] = z
        m_old = m_ref[hd]
        m_new = jnp.maximum(m_old, jnp.max(z, axis=0, keepdims=True))
        al_ref[hd] = jnp.exp2(m_old - m_new)
        m_ref[hd] = m_new

    def scores_full(hd, j):
        p = hd // 2
        track_max(hd, _dot(kn_ref[j, :, p * LANES:(p + 1) * LANES], qm_ref[hd]) - _lanes(ck_ref[hd, j], t))

    def scores_own(hd, s):
        p = hd // 2
        c0 = s * tk
        z = (_dot(kn_ref[base + s, :, p * LANES:(p + 1) * LANES], qm_ref[hd, :, c0:])
             - _lanes(ck_ref[hd, base + s], t - c0))
        keep = (lax.broadcasted_iota(jnp.int32, (tk, t - c0), 0) <= lax.broadcasted_iota(jnp.int32, (tk, t - c0), 1))
        z = jnp.where(keep, z, NEG)
        if c0:
            z = jnp.concatenate([jnp.full((tk, c0), NEG, F32), z], axis=1)
        track_max(hd, z)

    def accumulate(hd, j):
        pr = jnp.exp2(z_ref[hd] - m_ref[hd]).astype(BF16)
        acc_ref[hd] = al_ref[hd] * acc_ref[hd] + _dot(vt_ref[hd, j], pr)

    ut = proj(OFF_CC, CONV_DIM) * proj(OFF_CH, CONV_DIM)
    prev = uprev_ref[...]
    yc = (_lanes(cw_ref[0], t) * _shift_lanes(ut, prev, 2) + _lanes(cw_ref[1], t) * _shift_lanes(ut, prev, 1)
          + _lanes(cw_ref[2], t) * ut)
    zn = _rms_t(proj(OFF_CB, CONV_DIM) * yc, cg_ref[...]) * _silu(proj(OFF_GC, CONV_DIM))
    uprev_ref[...] = ut[:, t - LANES:]
    zt_ref[...] = zn.astype(BF16).T
    gate_ref[...] = _silu(proj(OFF_GA, ATT_DIM))

    for hd in range(N_HEADS):
        scores_own(hd, 0)
    for s in range(1, n_sub):
        for hd in range(N_HEADS):
            accumulate(hd, base + s - 1)
            scores_own(hd, s)

    def body(j, carry):
        prev = jnp.where(j == 0, base + n_sub - 1, j - 1)
        for hd in range(N_HEADS):
            accumulate(hd, prev)
            scores_full(hd, j)
        return carry

    lax.fori_loop(0, base, body, 0)
    last = jnp.where(i == 0, n_sub - 1, base - 1)
    quarter = D_MODEL // 4
    for hd in range(N_HEADS):
        accumulate(hd, last)
        if hd % 2 == 1:
            cs = slice((hd // 2) * quarter, (hd // 2 + 1) * quarter)
            y_ref[0, :, cs] = x_ref[0, :, cs] + _dot(zt_ref[...], wo_ref[ATT_DIM:, cs])

    outs = []
    for hd in range(N_HEADS):
        a = acc_ref[hd]
        outs.append(a[:HEAD_DIM] / a[HEAD_DIM:HEAD_DIM + 1])
    att = jnp.concatenate(outs, axis=0)
    an = (_rms_t(att, ag_ref[...]) * gate_ref[...]).astype(BF16).T
    y_ref[0] = y_ref[0] + out_proj(an, 0)

    @pl.when(i == nt - 1)
    def _():
        co_ref[0] = uprev_ref[...].T[LANES - (CONV_W - 1):, :]


def _const_spec(shape):
    nd = len(shape)
    return pl.BlockSpec(shape, lambda *_: (0,) * nd)


def _prompt_layer(layer, depth, x, consts, stacked):
    b, s, d = x.shape
    t = SEQ_TILE
    tk = KEY_TILE
    nt = s // t
    nk = s // tk
    any_spec = pl.BlockSpec(memory_space=pl.ANY)
    n_in = 1 + len(consts)
    return pl.pallas_call(
        functools.partial(_prompt_kernel, len(stacked)),
        grid=(b, nt),
        in_specs=[pl.BlockSpec((1, t, d), lambda bi, ti: (bi, ti, 0))]
                 + [_const_spec(a.shape) for a in consts] + [any_spec] * len(stacked),
        out_specs=[pl.BlockSpec((1, t, d), lambda bi, ti: (bi, ti, 0)),
                   pl.BlockSpec((1, 1, N_HEADS, HEAD_DIM, t), lambda bi, ti: (layer, bi, 0, 0, ti)),
                   pl.BlockSpec((1, 1, N_HEADS, HEAD_DIM, t), lambda bi, ti: (layer, bi, 0, 0, ti)),
                   pl.BlockSpec((1, 1, N_HEADS, t), lambda bi, ti: (layer, bi, 0, ti)),
                   pl.BlockSpec((1, CONV_W - 1, CONV_DIM), lambda bi, ti: (bi, 0, 0))],
        out_shape=[jax.ShapeDtypeStruct((b, s, d), F32),
                   jax.ShapeDtypeStruct((depth, b, N_HEADS, HEAD_DIM, s), F32),
                   jax.ShapeDtypeStruct((depth, b, N_HEADS, HEAD_DIM, s), F32),
                   jax.ShapeDtypeStruct((depth, b, N_HEADS, s), F32),
                   jax.ShapeDtypeStruct((b, CONV_W - 1, CONV_DIM), F32)],
        input_output_aliases={n_in + a: 1 + a for a in range(len(stacked))},
        scratch_shapes=[
            pltpu.VMEM((nk, tk, ATT_DIM), BF16),
            pltpu.VMEM((N_HEADS, nk, PV_ROWS, tk), BF16),
            pltpu.VMEM((N_HEADS, nk, tk, LANES), F32),
            pltpu.VMEM((N_HEADS, LANES, t), BF16),
            pltpu.VMEM((N_HEADS, tk, t), F32),
            pltpu.VMEM((N_HEADS, 1, t), F32),
            pltpu.VMEM((N_HEADS, 1, t), F32),
            pltpu.VMEM((N_HEADS, PV_ROWS, t), F32),
            pltpu.VMEM((N_HEADS, LANES), F32),
            pltpu.VMEM((CONV_DIM, LANES), F32),
            pltpu.VMEM((ATT_DIM, t), F32),
            pltpu.VMEM((t, CONV_DIM), BF16),
        ],
        compiler_params=pltpu.CompilerParams(
            dimension_semantics=("arbitrary", "arbitrary"), vmem_limit_bytes=VMEM_LIMIT),
        name="prompt_layer",
    )(x, *consts, *stacked)


def _rms(x, g):
    ms = jnp.mean(x * x, axis=-1, keepdims=True)
    return x * lax.rsqrt(ms + EPS) * g


def _head_rms(x, g_tiled, bd):
    ss = _dot((x * x).astype(BF16), bd)
    return x * lax.rsqrt(ss * (1.0 / HEAD_DIM) + EPS) * g_tiled


def _shift_rows(u, prev8, k):
    r = pltpu.roll(u, k, axis=0)
    pr = pltpu.roll(prev8, k, axis=0)
    rowid = lax.broadcasted_iota(jnp.int32, prev8.shape, 0)
    head = jnp.where(rowid < k, pr, r[0:SUBLANES])
    return jnp.concatenate([head, r[SUBLANES:]], axis=0)


def _sample_kernel(x_ref, pk_ref, pv_ref, pf_ref, st_ref,
                   ng_ref, w_ref, bf_ref, qg_ref, kg_ref, cw_ref, ag_ref, cg_ref, wo_ref,
                   bd_ref, tri_ref, upper_ref, ones_ref, blk_ref,
                   y_ref, ko_ref, vo_ref, fo_ref, co_ref):
    n = x_ref.shape[1]
    n_blk = pf_ref.shape[1] // N_HEADS
    past = pk_ref.shape[-1]
    x = x_ref[0]
    h = _rms(x, ng_ref[...]).astype(BF16)
    bd = bd_ref[...]
    q = _head_rms(_dot(h, w_ref[:, OFF_Q:OFF_Q + ATT_DIM]), qg_ref[...], bd) * (ATTN_SCALE * LOG2E)
    k = _head_rms(_dot(h, w_ref[:, OFF_K:OFF_K + ATT_DIM]), kg_ref[...], bd)
    v = _dot(h, w_ref[:, OFF_V:OFF_V + ATT_DIM])
    logf = _log_sigmoid(_dot(h, w_ref[:, OFF_F:OFF_F + F_PAD]) + bf_ref[...])
    ko_ref[0] = k
    vo_ref[0] = v
    fo_ref[0] = logf[:, :N_HEADS]

    pf = pf_ref[0]
    tot = _dot_exact_lhs(pf, ones_ref[...])
    before = _dot_exact_rhs(blk_ref[...], tot)
    c_past = _dot_exact_lhs(pf, upper_ref[...]) + before
    ck_past = jnp.concatenate(
        [c_past[bi * N_HEADS:(bi + 1) * N_HEADS] for bi in range(n_blk)], axis=1) * LOG2E
    past_total = (before + tot)[(n_blk - 1) * N_HEADS:, :LANES]

    zpad = jnp.zeros((LANES - n, F_PAD), F32)
    c_new = _dot_exact_rhs(tri_ref[...], jnp.concatenate([logf, zpad], axis=0))
    ck_new = (c_new.T[:N_HEADS, :] + past_total) * LOG2E

    row = lax.broadcasted_iota(jnp.int32, (n, LANES), 0)
    col = lax.broadcasted_iota(jnp.int32, (n, LANES), 1)
    keep = col <= row
    low = lax.broadcasted_iota(jnp.int32, (n, LANES), 1) < HEAD_DIM
    cols = []
    for p in range(N_PAIRS):
        sl = slice(p * LANES, (p + 1) * LANES)
        kpt = pk_ref[0, 0, 2 * p:2 * p + 2].reshape(LANES, past).astype(BF16)
        vpt = pv_ref[0, 0, 2 * p:2 * p + 2].reshape(LANES, past).astype(BF16)
        zrows = jnp.zeros((LANES - n, LANES), F32)
        kn = jnp.concatenate([k[:, sl], zrows], axis=0).astype(BF16)
        vn = jnp.concatenate([v[:, sl], zrows], axis=0).astype(BF16)
        o = []
        for hd in (2 * p, 2 * p + 1):
            keep_q = low if hd % 2 == 0 else jnp.logical_not(low)
            qm = jnp.where(keep_q, q[:, sl], 0.0).astype(BF16)
            z_past = _dot(qm, kpt) - ck_past[hd:hd + 1, :]
            z_new = jnp.where(keep, _dot_nt(qm, kn) - ck_new[hd:hd + 1, :], NEG)
            m = jnp.maximum(jnp.max(z_past, axis=-1, keepdims=True), jnp.max(z_new, axis=-1, keepdims=True))
            p_past = jnp.exp2(z_past - m)
            p_new = jnp.exp2(z_new - m)
            l = jnp.sum(p_past, axis=-1, keepdims=True) + jnp.sum(p_new, axis=-1, keepdims=True)
            o.append((_dot_nt(p_past.astype(BF16), vpt) + _dot(p_new.astype(BF16), vn)) / l)
        cols.append(jnp.where(low, o[0], o[1]))
    att = jnp.concatenate(cols, axis=1)

    st = st_ref[0, 0]
    u_prev8 = jnp.concatenate([jnp.zeros((SUBLANES - (CONV_W - 1), CONV_DIM), F32), st], axis=0)
    cc = _dot(h, w_ref[:, OFF_CC:OFF_CC + CONV_DIM])
    ch = _dot(h, w_ref[:, OFF_CH:OFF_CH + CONV_DIM])
    u = cc * ch
    cw = cw_ref[...]
    yc = cw[0:1] * _shift_rows(u, u_prev8, 2) + cw[1:2] * _shift_rows(u, u_prev8, 1) + cw[2:3] * u
    z = _dot(h, w_ref[:, OFF_CB:OFF_CB + CONV_DIM]) * yc
    zn = _rms(z, cg_ref[...]) * _silu(_dot(h, w_ref[:, OFF_GC:OFF_GC + CONV_DIM]))
    an = _rms(att, ag_ref[...]) * _silu(_dot(h, w_ref[:, OFF_GA:OFF_GA + ATT_DIM]))
    mix = jnp.concatenate([an, zn], axis=1).astype(BF16)
    y_ref[0] = x + _dot(mix, wo_ref[...])
    co_ref[0] = u[n - (CONV_W - 1):, :]


def _sample_layer(layer, x, pk_t, pv_t, pf, st_all, consts):
    b, n, d = x.shape
    past = pk_t.shape[-1]
    per_b = lambda a: pl.BlockSpec((1,) + a.shape[1:], lambda bi: (bi,) + (0,) * (a.ndim - 1))
    per_lb = lambda a: pl.BlockSpec((1, 1) + a.shape[2:], lambda bi: (layer, bi) + (0,) * (a.ndim - 2))
    out_shape = [jax.ShapeDtypeStruct((b, n, d), F32),
                 jax.ShapeDtypeStruct((b, n, ATT_DIM), F32),
                 jax.ShapeDtypeStruct((b, n, ATT_DIM), F32),
                 jax.ShapeDtypeStruct((b, n, N_HEADS), F32),
                 jax.ShapeDtypeStruct((b, CONV_W - 1, CONV_DIM), F32)]
    return pl.pallas_call(
        _sample_kernel,
        grid=(b,),
        in_specs=[per_b(x), per_lb(pk_t), per_lb(pv_t), per_b(pf), per_lb(st_all)]
                 + [_const_spec(a.shape) for a in consts],
        out_specs=[per_b(a) for a in out_shape],
        out_shape=out_shape,
        compiler_params=pltpu.CompilerParams(
            dimension_semantics=("arbitrary",), vmem_limit_bytes=VMEM_LIMIT),
        name="sample_layer",
    )(x, pk_t, pv_t, pf, st_all, *consts)


def _rep(v):
    return jnp.broadcast_to(v[..., None], v.shape + (LANES,))


def kernel(x_prompt, x_sample, cache_k, cache_v, cache_logf, state_conv, norm_g, w_in, b_f,
           q_norm_g, k_norm_g, conv_w, att_out_g, conv_out_g, w_out):
    depth = w_in.shape[0]
    bp, sp, _ = x_prompt.shape
    bs, ns, _ = x_sample.shape
    past = cache_k.shape[2]
    n_blk = past // PAST_BLK

    bd = jnp.asarray(np.kron(np.eye(N_HEADS, dtype=np.float32), np.ones((HEAD_DIM, HEAD_DIM), np.float32)), BF16)
    up_t = jnp.asarray(np.triu(np.ones((SEQ_TILE, SEQ_TILE), np.float32)), BF16)
    tri_l = jnp.asarray(np.tril(np.ones((LANES, LANES), np.float32)), BF16)
    upper = jnp.asarray(np.triu(np.ones((PAST_BLK, PAST_BLK), np.float32)), BF16)
    ones = jnp.ones((PAST_BLK, PAST_BLK), BF16)
    r = np.arange(n_blk * N_HEADS)
    blk = jnp.asarray(((r[:, None] % N_HEADS == r[None, :] % N_HEADS)
                       & (r[None, :] // N_HEADS < r[:, None] // N_HEADS)).astype(np.float32), BF16)

    pk_t = jnp.transpose(cache_k, (0, 1, 3, 4, 2))
    pv_t = jnp.transpose(cache_v, (0, 1, 3, 4, 2))

    f0 = 4 * ATT_DIM
    stacked = ()
    yp, ys = x_prompt, x_sample
    outs = [[] for _ in range(5)]
    for l in range(depth):
        w_l = w_in[l]
        w_nat = jnp.concatenate([w_l[:, :f0], w_l[:, f0 + N_HEADS:], w_l[:, f0:f0 + N_HEADS],
                                 jnp.zeros((D_MODEL, F_PAD - N_HEADS), F32)], axis=1).astype(BF16)
        w_t = jnp.transpose(w_l)
        w_t = jnp.concatenate([w_t[:f0], w_t[f0 + N_HEADS:], w_t[f0:f0 + N_HEADS],
                               jnp.zeros((F_ROWS - N_HEADS, D_MODEL), F32)], axis=0).astype(BF16)
        wo = w_out[l].astype(BF16)
        ng = norm_g[l][None, :]

        p_consts = (ng, w_t, wo, _rep(b_f[l]), _rep(q_norm_g[l]), _rep(k_norm_g[l]), _rep(conv_w[l]),
                    _rep(att_out_g[l]), _rep(conv_out_g[l]), up_t)
        yp, *stacked, c1 = _prompt_layer(l, depth, yp, p_consts, tuple(stacked))

        bfp = jnp.concatenate([b_f[l], jnp.zeros((F_PAD - N_HEADS,), F32)])[None, :]
        s_consts = (ng, w_nat, bfp, jnp.tile(q_norm_g[l], N_HEADS)[None, :], jnp.tile(k_norm_g[l], N_HEADS)[None, :],
                    conv_w[l], att_out_g[l][None, :], conv_out_g[l][None, :], wo,
                    bd, tri_l, upper, ones, blk)
        pf = cache_logf[l].reshape(bs, n_blk, PAST_BLK, N_HEADS).transpose(0, 1, 3, 2)
        pf = pf.reshape(bs, n_blk * N_HEADS, PAST_BLK)
        ys, k2, v2, f2, c2 = _sample_layer(l, ys, pk_t, pv_t, pf, state_conv, s_consts)
        for lst, a in zip(outs, (c1, k2.reshape(bs, ns, N_HEADS, HEAD_DIM), v2.reshape(bs, ns, N_HEADS, HEAD_DIM),
                                 f2, c2)):
            lst.append(a)
    k_all, v_all, f_all = stacked
    k_prompt = jnp.transpose(k_all, (0, 1, 4, 2, 3))
    v_prompt = jnp.transpose(v_all, (0, 1, 4, 2, 3))
    f_prompt = jnp.transpose(f_all, (0, 1, 3, 2))
    c_prompt, k_s, v_s, f_s, c_s = (jnp.stack(o) for o in outs)
    return (yp, ys, k_prompt, v_prompt, f_prompt, c_prompt, k_s, v_s, f_s, c_s)
```

```python
import functools

import numpy as np
import jax
import jax.numpy as jnp
from jax import lax
from jax.experimental import pallas as pl
from jax.experimental.pallas import tpu as pltpu

D_MODEL = 1024
HEAD_DIM = 64
N_HEADS = 8
N_PAIRS = N_HEADS // 2
ATT_DIM = N_HEADS * HEAD_DIM
CONV_DIM = 512
CONV_W = 3
EPS = 1e-6
NEG = -1e30
ATTN_SCALE = HEAD_DIM ** -0.5
LOG2E = 1.4426950408889634

LANES = 128
SUBLANES = 8
BF16_ROWS = 16
F_PAD = LANES
F_ROWS = BF16_ROWS
OFF_Q, OFF_K, OFF_V, OFF_GA = 0, 512, 1024, 1536
OFF_CB, OFF_CC, OFF_CH, OFF_GC = 2048, 2560, 3072, 3584
OFF_F = 4096
ROW_Q, ROW_K = 0, 512
ROW_F = ROW_K + ATT_DIM
ROW_V = ROW_F + F_ROWS
ROW_GA = ROW_V + ATT_DIM
ROW_CB = ROW_GA + ATT_DIM
ROW_CC, ROW_CH, ROW_GC = ROW_CB + CONV_DIM, ROW_CB + 2 * CONV_DIM, ROW_CB + 3 * CONV_DIM

SEQ_TILE = 512
KEY_TILE = 256
PV_ROWS = HEAD_DIM + BF16_ROWS
PAST_BLK = 256
VMEM_LIMIT = 58 * 1024 * 1024

BF16 = jnp.bfloat16
F32 = jnp.float32


def _dot(a, b):
    return jnp.dot(a, b, preferred_element_type=F32)


def _dot_nt(a, b):
    return lax.dot_general(a, b, (((1,), (1,)), ((), ())), preferred_element_type=F32)


def _split3(x):
    hi = x.astype(BF16)
    r = x - hi.astype(F32)
    mid = r.astype(BF16)
    lo = (r - mid.astype(F32)).astype(BF16)
    return hi, mid, lo


def _dot_exact_rhs(m01, x):
    hi, mid, lo = _split3(x)
    return _dot(m01, hi) + _dot(m01, mid) + _dot(m01, lo)


def _dot_exact_lhs(x, m01):
    hi, mid, lo = _split3(x)
    return _dot(hi, m01) + _dot(mid, m01) + _dot(lo, m01)


def _silu(x):
    return x / (1.0 + jnp.exp(-x))


def _log_sigmoid(x):
    return jnp.minimum(x, 0.0) - jnp.log1p(jnp.exp(-jnp.abs(x)))


def _lanes(rep, t):
    return jnp.concatenate([rep] * (t // LANES), axis=-1)


def _rms_t(xt, g_rep):
    t = xt.shape[1]
    ms = jnp.mean(xt * xt, axis=0, keepdims=True)
    return xt * lax.rsqrt(ms + EPS) * _lanes(g_rep, t)


def _head_rms_t(xt, g_rep, eps_row):
    t = xt.shape[1]
    x3 = xt.reshape(N_HEADS, HEAD_DIM, t)
    ms = jnp.mean(x3 * x3, axis=1, keepdims=True)
    return (x3 * lax.rsqrt(ms + eps_row[None]) * _lanes(g_rep, t)[None]).reshape(ATT_DIM, t)


def _shift_lanes(ut, prev, k):
    r = pltpu.roll(ut, k, axis=1)
    pr = pltpu.roll(prev, k, axis=1)
    lane = lax.broadcasted_iota(jnp.int32, prev.shape, 1)
    head = jnp.where(lane < k, pr, r[:, :LANES])
    return jnp.concatenate([head, r[:, LANES:]], axis=1)


def _prompt_kernel(n_aliased, x_ref, wt_ref, wo_ref, bf_ref, qg_ref, kg_ref, cw_ref, ag_ref, cg_ref,
                   up_ref, *refs):
    (y_ref, ko_ref, vo_ref, fo_ref, co_ref,
     kn_ref, vt_ref, ck_ref, qm_ref, z_ref, m_ref, al_ref, acc_ref, carry_ref, uprev_ref,
     gate_ref, zt_ref) = refs[n_aliased:]
    i = pl.program_id(1)
    nt = pl.num_programs(1)
    t = SEQ_TILE
    tk = KEY_TILE
    n_sub = t // tk
    base = n_sub * i

    @pl.when(i == 0)
    def _():
        carry_ref[...] = jnp.zeros_like(carry_ref)
        uprev_ref[...] = jnp.zeros_like(uprev_ref)

    x = x_ref[0]
    ht = x.astype(BF16).T
    ms = jnp.mean(x * x, axis=-1, keepdims=True) + EPS
    ms_row = jnp.broadcast_to(ms, (t, LANES)).T[:1]
    r_row = lax.rsqrt(ms_row)
    eps_row = EPS * ms_row

    def proj(off, n):
        half = (n // 2) // BF16_ROWS * BF16_ROWS
        return jnp.concatenate([_dot(wt_ref[off:off + half, :], ht),
                                _dot(wt_ref[off + half:off + n, :], ht)], axis=0)

    def out_proj(a, row0):
        half = D_MODEL // 2
        return jnp.concatenate([_dot(a, wo_ref[row0:row0 + ATT_DIM, :half]),
                                _dot(a, wo_ref[row0:row0 + ATT_DIM, half:])], axis=1)

    kf = proj(ROW_K, ATT_DIM + F_ROWS)
    logf = _log_sigmoid(kf[ATT_DIM:] * r_row + _lanes(bf_ref[...], t))
    fo_ref[0, 0] = logf[:N_HEADS]

    kt = _head_rms_t(kf[:ATT_DIM], kg_ref[...], eps_row)
    ko_ref[0, 0] = kt.reshape(N_HEADS, HEAD_DIM, t)
    kn = kt.astype(BF16).T
    for s in range(n_sub):
        kn_ref[base + s] = kn[s * tk:(s + 1) * tk]
    vt = proj(ROW_V, ATT_DIM) * r_row
    vo_ref[0, 0] = vt.reshape(N_HEADS, HEAD_DIM, t)
    ones_rows = jnp.ones((PV_ROWS - HEAD_DIM, tk), BF16)
    for hd in range(N_HEADS):
        v_h = vt[hd * HEAD_DIM:(hd + 1) * HEAD_DIM].astype(BF16)
        for s in range(n_sub):
            vt_ref[hd, base + s] = jnp.concatenate([v_h[:, s * tk:(s + 1) * tk], ones_rows], axis=0)

    qt = (_head_rms_t(proj(ROW_Q, ATT_DIM), qg_ref[...], eps_row) * (ATTN_SCALE * LOG2E)).astype(BF16)
    zero_rows = jnp.zeros((HEAD_DIM, t), BF16)
    for hd in range(N_HEADS):
        q_h = qt[hd * HEAD_DIM:(hd + 1) * HEAD_DIM]
        qm_ref[hd] = jnp.concatenate([q_h, zero_rows] if hd % 2 == 0 else [zero_rows, q_h], axis=0)

    c3 = _dot(jnp.concatenate(_split3(logf), axis=0), up_ref[...])
    c = (c3[:F_ROWS] + c3[F_ROWS:2 * F_ROWS] + c3[2 * F_ROWS:])[:N_HEADS] + _lanes(carry_ref[...], t)
    carry_ref[...] = jnp.broadcast_to(c[:, t - 1:], carry_ref.shape)
    c_pos = jnp.concatenate([c, jnp.zeros((LANES - N_HEADS, t), F32)], axis=0).T
    for hd in range(N_HEADS):
        c_rep = jnp.broadcast_to(c_pos[:, hd:hd + 1] * LOG2E, (t, LANES))
        for s in range(n_sub):
            ck_ref[hd, base + s] = c_rep[s * tk:(s + 1) * tk]

    m_ref[...] = jnp.full(m_ref.shape, NEG, F32)
    acc_ref[...] = jnp.zeros_like(acc_ref)

    def track_max(hd, z, c0):
        z_ref[hd, :, c0:] = z
        m_old = m_ref[hd, :, c0:]
        m_new = jnp.maximum(m_old, jnp.max(z, axis=0, keepdims=True))
        al_ref[hd, :, c0:] = jnp.exp2(m_old - m_new)
        m_ref[hd, :, c0:] = m_new

    def scores_full(hd, j):
        p = hd // 2
        track_max(hd, _dot(kn_ref[j, :, p * LANES:(p + 1) * LANES], qm_ref[hd]) - _lanes(ck_ref[hd, j], t), 0)

    def scores_own(hd, s):
        p = hd // 2
        c0 = s * tk
        z = (_dot(kn_ref[base + s, :, p * LANES:(p + 1) * LANES], qm_ref[hd, :, c0:])
             - _lanes(ck_ref[hd, base + s], t - c0))
        keep = lax.broadcasted_iota(jnp.int32, (tk, tk), 0) <= lax.broadcasted_iota(jnp.int32, (tk, tk), 1)
        diag = jnp.where(keep, z[:, :tk], NEG)
        z = diag if t - c0 == tk else jnp.concatenate([diag, z[:, tk:]], axis=1)
        track_max(hd, z, c0)

    def accumulate(hd, j, c0=0):
        pr = jnp.exp2(z_ref[hd, :, c0:] - m_ref[hd, :, c0:]).astype(BF16)
        acc_ref[hd, :, c0:] = al_ref[hd, :, c0:] * acc_ref[hd, :, c0:] + _dot(vt_ref[hd, j], pr)

    ut = proj(ROW_CC, CONV_DIM) * proj(ROW_CH, CONV_DIM) * (r_row * r_row)
    prev = uprev_ref[...]
    yc = (_lanes(cw_ref[0], t) * _shift_lanes(ut, prev, 2) + _lanes(cw_ref[1], t) * _shift_lanes(ut, prev, 1)
          + _lanes(cw_ref[2], t) * ut)
    zn = _rms_t(proj(ROW_CB, CONV_DIM) * r_row * yc, cg_ref[...]) * _silu(proj(ROW_GC, CONV_DIM) * r_row)
    uprev_ref[...] = ut[:, t - LANES:]
    zt_ref[...] = zn.astype(BF16).T
    gate_ref[...] = _silu(proj(ROW_GA, ATT_DIM) * r_row)

    for hd in range(N_HEADS):
        scores_own(hd, n_sub - 1)
    for s in range(n_sub - 2, -1, -1):
        for hd in range(N_HEADS):
            accumulate(hd, base + s + 1, (s + 1) * tk)
            scores_own(hd, s)

    def body(g, carry):
        first = n_sub * g
        prev = jnp.where(g == 0, base, first - 1)
        for s in range(n_sub):
            for hd in range(N_HEADS):
                accumulate(hd, prev if s == 0 else first + s - 1)
                scores_full(hd, first + s)
        return carry

    lax.fori_loop(0, i, body, 0)
    last = jnp.where(i == 0, base, base - 1)
    quarter = D_MODEL // 4
    for hd in range(N_HEADS):
        accumulate(hd, last)
        if hd % 2 == 1:
            cs = slice((hd // 2) * quarter, (hd // 2 + 1) * quarter)
            y_ref[0, :, cs] = x_ref[0, :, cs] + _dot(zt_ref[...], wo_ref[ATT_DIM:, cs])

    outs = []
    for hd in range(N_HEADS):
        a = acc_ref[hd]
        outs.append(a[:HEAD_DIM] / a[HEAD_DIM:HEAD_DIM + 1])
    att = jnp.concatenate(outs, axis=0)
    an = (_rms_t(att, ag_ref[...]) * gate_ref[...]).astype(BF16).T
    y_ref[0] = y_ref[0] + out_proj(an, 0)

    @pl.when(i == nt - 1)
    def _():
        co_ref[0] = uprev_ref[...].T[LANES - (CONV_W - 1):, :]


def _const_spec(shape):
    nd = len(shape)
    return pl.BlockSpec(shape, lambda *_: (0,) * nd)


def _prompt_layer(layer, depth, x, consts, stacked):
    b, s, d = x.shape
    t = SEQ_TILE
    tk = KEY_TILE
    nt = s // t
    nk = s // tk
    any_spec = pl.BlockSpec(memory_space=pl.ANY)
    n_in = 1 + len(consts)
    return pl.pallas_call(
        functools.partial(_prompt_kernel, len(stacked)),
        grid=(b, nt),
        in_specs=[pl.BlockSpec((1, t, d), lambda bi, ti: (bi, ti, 0))]
                 + [_const_spec(a.shape) for a in consts] + [any_spec] * len(stacked),
        out_specs=[pl.BlockSpec((1, t, d), lambda bi, ti: (bi, ti, 0)),
                   pl.BlockSpec((1, 1, N_HEADS, HEAD_DIM, t), lambda bi, ti: (layer, bi, 0, 0, ti)),
                   pl.BlockSpec((1, 1, N_HEADS, HEAD_DIM, t), lambda bi, ti: (layer, bi, 0, 0, ti)),
                   pl.BlockSpec((1, 1, N_HEADS, t), lambda bi, ti: (layer, bi, 0, ti)),
                   pl.BlockSpec((1, CONV_W - 1, CONV_DIM), lambda bi, ti: (bi, 0, 0))],
        out_shape=[jax.ShapeDtypeStruct((b, s, d), F32),
                   jax.ShapeDtypeStruct((depth, b, N_HEADS, HEAD_DIM, s), F32),
                   jax.ShapeDtypeStruct((depth, b, N_HEADS, HEAD_DIM, s), F32),
                   jax.ShapeDtypeStruct((depth, b, N_HEADS, s), F32),
                   jax.ShapeDtypeStruct((b, CONV_W - 1, CONV_DIM), F32)],
        input_output_aliases={n_in + a: 1 + a for a in range(len(stacked))},
        scratch_shapes=[
            pltpu.VMEM((nk, tk, ATT_DIM), BF16),
            pltpu.VMEM((N_HEADS, nk, PV_ROWS, tk), BF16),
            pltpu.VMEM((N_HEADS, nk, tk, LANES), F32),
            pltpu.VMEM((N_HEADS, LANES, t), BF16),
            pltpu.VMEM((N_HEADS, tk, t), F32),
            pltpu.VMEM((N_HEADS, 1, t), F32),
            pltpu.VMEM((N_HEADS, 1, t), F32),
            pltpu.VMEM((N_HEADS, PV_ROWS, t), F32),
            pltpu.VMEM((N_HEADS, LANES), F32),
            pltpu.VMEM((CONV_DIM, LANES), F32),
            pltpu.VMEM((ATT_DIM, t), F32),
            pltpu.VMEM((t, CONV_DIM), BF16),
        ],
        compiler_params=pltpu.CompilerParams(
            dimension_semantics=("arbitrary", "arbitrary"), vmem_limit_bytes=VMEM_LIMIT),
        name="prompt_layer",
    )(x, *consts, *stacked)


def _rms(x, g):
    ms = jnp.mean(x * x, axis=-1, keepdims=True)
    return x * lax.rsqrt(ms + EPS) * g


def _head_rms(x, g_tiled, bd):
    ss = _dot((x * x).astype(BF16), bd)
    return x * lax.rsqrt(ss * (1.0 / HEAD_DIM) + EPS) * g_tiled


def _shift_rows(u, prev8, k):
    r = pltpu.roll(u, k, axis=0)
    pr = pltpu.roll(prev8, k, axis=0)
    rowid = lax.broadcasted_iota(jnp.int32, prev8.shape, 0)
    head = jnp.where(rowid < k, pr, r[0:SUBLANES])
    return jnp.concatenate([head, r[SUBLANES:]], axis=0)


def _sample_kernel(x_ref, pk_ref, pv_ref, pf_ref, st_ref,
                   ng_ref, w_ref, bf_ref, qg_ref, kg_ref, cw_ref, ag_ref, cg_ref, wo_ref,
                   bd_ref, tri_ref, upper_ref, ones_ref, blk_ref,
                   y_ref, ko_ref, vo_ref, fo_ref, co_ref):
    n = x_ref.shape[1]
    n_blk = pf_ref.shape[1] // N_HEADS
    past = pk_ref.shape[-1]
    x = x_ref[0]
    h = _rms(x, ng_ref[...]).astype(BF16)
    bd = bd_ref[...]
    q = _head_rms(_dot(h, w_ref[:, OFF_Q:OFF_Q + ATT_DIM]), qg_ref[...], bd) * (ATTN_SCALE * LOG2E)
    k = _head_rms(_dot(h, w_ref[:, OFF_K:OFF_K + ATT_DIM]), kg_ref[...], bd)
    v = _dot(h, w_ref[:, OFF_V:OFF_V + ATT_DIM])
    logf = _log_sigmoid(_dot(h, w_ref[:, OFF_F:OFF_F + F_PAD]) + bf_ref[...])
    ko_ref[0] = k
    vo_ref[0] = v
    fo_ref[0] = logf[:, :N_HEADS]

    pf = pf_ref[0]
    tot = _dot_exact_lhs(pf, ones_ref[...])
    before = _dot_exact_rhs(blk_ref[...], tot)
    c_past = _dot_exact_lhs(pf, upper_ref[...]) + before
    ck_past = jnp.concatenate(
        [c_past[bi * N_HEADS:(bi + 1) * N_HEADS] for bi in range(n_blk)], axis=1) * LOG2E
    past_total = (before + tot)[(n_blk - 1) * N_HEADS:, :LANES]

    zpad = jnp.zeros((LANES - n, F_PAD), F32)
    c_new = _dot_exact_rhs(tri_ref[...], jnp.concatenate([logf, zpad], axis=0))
    ck_new = (c_new.T[:N_HEADS, :] + past_total) * LOG2E

    row = lax.broadcasted_iota(jnp.int32, (n, LANES), 0)
    col = lax.broadcasted_iota(jnp.int32, (n, LANES), 1)
    keep = col <= row
    low = lax.broadcasted_iota(jnp.int32, (n, LANES), 1) < HEAD_DIM
    cols = []
    for p in range(N_PAIRS):
        sl = slice(p * LANES, (p + 1) * LANES)
        kpt = pk_ref[0, 0, 2 * p:2 * p + 2].reshape(LANES, past).astype(BF16)
        vpt = pv_ref[0, 0, 2 * p:2 * p + 2].reshape(LANES, past).astype(BF16)
        zrows = jnp.zeros((LANES - n, LANES), F32)
        kn = jnp.concatenate([k[:, sl], zrows], axis=0).astype(BF16)
        vn = jnp.concatenate([v[:, sl], zrows], axis=0).astype(BF16)
        o = []
        for hd in (2 * p, 2 * p + 1):
            keep_q = low if hd % 2 == 0 else jnp.logical_not(low)
            qm = jnp.where(keep_q, q[:, sl], 0.0).astype(BF16)
            z_past = _dot(qm, kpt) - ck_past[hd:hd + 1, :]
            z_new = jnp.where(keep, _dot_nt(qm, kn) - ck_new[hd:hd + 1, :], NEG)
            m = jnp.maximum(jnp.max(z_past, axis=-1, keepdims=True), jnp.max(z_new, axis=-1, keepdims=True))
            p_past = jnp.exp2(z_past - m)
            p_new = jnp.exp2(z_new - m)
            l = jnp.sum(p_past, axis=-1, keepdims=True) + jnp.sum(p_new, axis=-1, keepdims=True)
            o.append((_dot_nt(p_past.astype(BF16), vpt) + _dot(p_new.astype(BF16), vn)) / l)
        cols.append(jnp.where(low, o[0], o[1]))
    att = jnp.concatenate(cols, axis=1)

    st = st_ref[0, 0]
    u_prev8 = jnp.concatenate([jnp.zeros((SUBLANES - (CONV_W - 1), CONV_DIM), F32), st], axis=0)
    cc = _dot(h, w_ref[:, OFF_CC:OFF_CC + CONV_DIM])
    ch = _dot(h, w_ref[:, OFF_CH:OFF_CH + CONV_DIM])
    u = cc * ch
    cw = cw_ref[...]
    yc = cw[0:1] * _shift_rows(u, u_prev8, 2) + cw[1:2] * _shift_rows(u, u_prev8, 1) + cw[2:3] * u
    z = _dot(h, w_ref[:, OFF_CB:OFF_CB + CONV_DIM]) * yc
    zn = _rms(z, cg_ref[...]) * _silu(_dot(h, w_ref[:, OFF_GC:OFF_GC + CONV_DIM]))
    an = _rms(att, ag_ref[...]) * _silu(_dot(h, w_ref[:, OFF_GA:OFF_GA + ATT_DIM]))
    mix = jnp.concatenate([an, zn], axis=1).astype(BF16)
    y_ref[0] = x + _dot(mix, wo_ref[...])
    co_ref[0] = u[n - (CONV_W - 1):, :]


def _sample_layer(layer, x, pk_t, pv_t, pf, st_all, consts):
    b, n, d = x.shape
    past = pk_t.shape[-1]
    per_b = lambda a: pl.BlockSpec((1,) + a.shape[1:], lambda bi: (bi,) + (0,) * (a.ndim - 1))
    per_lb = lambda a: pl.BlockSpec((1, 1) + a.shape[2:], lambda bi: (layer, bi) + (0,) * (a.ndim - 2))
    out_shape = [jax.ShapeDtypeStruct((b, n, d), F32),
                 jax.ShapeDtypeStruct((b, n, ATT_DIM), F32),
                 jax.ShapeDtypeStruct((b, n, ATT_DIM), F32),
                 jax.ShapeDtypeStruct((b, n, N_HEADS), F32),
                 jax.ShapeDtypeStruct((b, CONV_W - 1, CONV_DIM), F32)]
    return pl.pallas_call(
        _sample_kernel,
        grid=(b,),
        in_specs=[per_b(x), per_lb(pk_t), per_lb(pv_t), per_b(pf), per_lb(st_all)]
                 + [_const_spec(a.shape) for a in consts],
        out_specs=[per_b(a) for a in out_shape],
        out_shape=out_shape,
        compiler_params=pltpu.CompilerParams(
            dimension_semantics=("arbitrary",), vmem_limit_bytes=VMEM_LIMIT),
        name="sample_layer",
    )(x, pk_t, pv_t, pf, st_all, *consts)


def _rep(v):
    return jnp.broadcast_to(v[..., None], v.shape + (LANES,))


def kernel(x_prompt, x_sample, cache_k, cache_v, cache_logf, state_conv, norm_g, w_in, b_f,
           q_norm_g, k_norm_g, conv_w, att_out_g, conv_out_g, w_out):
    depth = w_in.shape[0]
    bp, sp, _ = x_prompt.shape
    bs, ns, _ = x_sample.shape
    past = cache_k.shape[2]
    n_blk = past // PAST_BLK

    bd = jnp.asarray(np.kron(np.eye(N_HEADS, dtype=np.float32), np.ones((HEAD_DIM, HEAD_DIM), np.float32)), BF16)
    up_t = jnp.asarray(np.triu(np.ones((SEQ_TILE, SEQ_TILE), np.float32)), BF16)
    tri_l = jnp.asarray(np.tril(np.ones((LANES, LANES), np.float32)), BF16)
    upper = jnp.asarray(np.triu(np.ones((PAST_BLK, PAST_BLK), np.float32)), BF16)
    ones = jnp.ones((PAST_BLK, PAST_BLK), BF16)
    r = np.arange(n_blk * N_HEADS)
    blk = jnp.asarray(((r[:, None] % N_HEADS == r[None, :] % N_HEADS)
                       & (r[None, :] // N_HEADS < r[:, None] // N_HEADS)).astype(np.float32), BF16)

    pk_t = jnp.transpose(cache_k, (0, 1, 3, 4, 2))
    pv_t = jnp.transpose(cache_v, (0, 1, 3, 4, 2))

    f0 = 4 * ATT_DIM
    stacked = ()
    yp, ys = x_prompt, x_sample
    outs = [[] for _ in range(5)]
    for l in range(depth):
        w_l = w_in[l]
        w_nat = jnp.concatenate([w_l[:, :f0], w_l[:, f0 + N_HEADS:], w_l[:, f0:f0 + N_HEADS],
                                 jnp.zeros((D_MODEL, F_PAD - N_HEADS), F32)], axis=1).astype(BF16)
        w_t = jnp.transpose(w_l) * norm_g[l][None, :]
        w_t = jnp.concatenate([w_t[:ROW_F], w_t[f0:f0 + N_HEADS], jnp.zeros((F_ROWS - N_HEADS, D_MODEL), F32),
                               w_t[ROW_F:f0], w_t[f0 + N_HEADS:]], axis=0).astype(BF16)
        bf_rows = _rep(jnp.concatenate([b_f[l], jnp.zeros((F_ROWS - N_HEADS,), F32)]))
        wo = w_out[l].astype(BF16)
        ng = norm_g[l][None, :]

        p_consts = (w_t, wo, bf_rows, _rep(q_norm_g[l]), _rep(k_norm_g[l]), _rep(conv_w[l]),
                    _rep(att_out_g[l]), _rep(conv_out_g[l]), up_t)
        yp, *stacked, c1 = _prompt_layer(l, depth, yp, p_consts, tuple(stacked))

        bfp = jnp.concatenate([b_f[l], jnp.zeros((F_PAD - N_HEADS,), F32)])[None, :]
        s_consts = (ng, w_nat, bfp, jnp.tile(q_norm_g[l], N_HEADS)[None, :], jnp.tile(k_norm_g[l], N_HEADS)[None, :],
                    conv_w[l], att_out_g[l][None, :], conv_out_g[l][None, :], wo,
                    bd, tri_l, upper, ones, blk)
        pf = cache_logf[l].reshape(bs, n_blk, PAST_BLK, N_HEADS).transpose(0, 1, 3, 2)
        pf = pf.reshape(bs, n_blk * N_HEADS, PAST_BLK)
        ys, k2, v2, f2, c2 = _sample_layer(l, ys, pk_t, pv_t, pf, state_conv, s_consts)
        for lst, a in zip(outs, (c1, k2.reshape(bs, ns, N_HEADS, HEAD_DIM), v2.reshape(bs, ns, N_HEADS, HEAD_DIM),
                                 f2, c2)):
            lst.append(a)
    k_all, v_all, f_all = stacked
    k_prompt = jnp.transpose(k_all, (0, 1, 4, 2, 3))
    v_prompt = jnp.transpose(v_all, (0, 1, 4, 2, 3))
    f_prompt = jnp.transpose(f_all, (0, 1, 3, 2))
    c_prompt, k_s, v_s, f_s, c_s = (jnp.stack(o) for o in outs)
    return (yp, ys, k_prompt, v_prompt, f_prompt, c_prompt, k_s, v_s, f_s, c_s)
```

```python
import functools

import numpy as np
import jax
import jax.numpy as jnp
from jax import lax
from jax.experimental import pallas as pl
from jax.experimental.pallas import tpu as pltpu

D_MODEL = 1024
HEAD_DIM = 64
N_HEADS = 8
N_PAIRS = N_HEADS // 2
ATT_DIM = N_HEADS * HEAD_DIM
CONV_DIM = 512
CONV_W = 3
EPS = 1e-6
NEG = -1e30
ATTN_SCALE = HEAD_DIM ** -0.5
LOG2E = 1.4426950408889634

LANES = 128
SUBLANES = 8
BF16_ROWS = 16
F_PAD = LANES
F_ROWS = BF16_ROWS
OFF_Q, OFF_K, OFF_V, OFF_GA = 0, 512, 1024, 1536
OFF_CB, OFF_CC, OFF_CH, OFF_GC = 0, 512, 1024, 1536

SEQ_TILE = 512
KEY_TILE = 256
PV_ROWS = HEAD_DIM + BF16_ROWS
PAST_BLK = 256
VMEM_LIMIT = 58 * 1024 * 1024

BF16 = jnp.bfloat16
F32 = jnp.float32


def _dot(a, b):
    return jnp.dot(a, b, preferred_element_type=F32)


def _dot_nt(a, b):
    return lax.dot_general(a, b, (((1,), (1,)), ((), ())), preferred_element_type=F32)


def _split3(x):
    hi = x.astype(BF16)
    r = x - hi.astype(F32)
    mid = r.astype(BF16)
    lo = (r - mid.astype(F32)).astype(BF16)
    return hi, mid, lo


def _dot_exact_rhs(m01, x):
    hi, mid, lo = _split3(x)
    return _dot(m01, hi) + _dot(m01, mid) + _dot(m01, lo)


def _dot_exact_lhs(x, m01):
    hi, mid, lo = _split3(x)
    return _dot(hi, m01) + _dot(mid, m01) + _dot(lo, m01)


def _silu(x):
    return x / (1.0 + jnp.exp(-x))


def _log_sigmoid(x):
    return jnp.minimum(x, 0.0) - jnp.log1p(jnp.exp(-jnp.abs(x)))


def _lanes(rep, t):
    return jnp.concatenate([rep] * (t // LANES), axis=-1)


def _rms_t(xt, g_rep):
    t = xt.shape[1]
    ms = jnp.mean(xt * xt, axis=0, keepdims=True)
    return xt * lax.rsqrt(ms + EPS) * _lanes(g_rep, t)


def _head_rms_t(xt, g_rep, eps_row):
    t = xt.shape[1]
    x3 = xt.reshape(N_HEADS, HEAD_DIM, t)
    ms = jnp.mean(x3 * x3, axis=1, keepdims=True)
    return (x3 * lax.rsqrt(ms + eps_row[None]) * _lanes(g_rep, t)[None]).reshape(ATT_DIM, t)


def _shift_lanes(ut, prev, k):
    r = pltpu.roll(ut, k, axis=1)
    pr = pltpu.roll(prev, k, axis=1)
    lane = lax.broadcasted_iota(jnp.int32, prev.shape, 1)
    head = jnp.where(lane < k, pr, r[:, :LANES])
    return jnp.concatenate([head, r[:, LANES:]], axis=1)


def _prompt_kernel(n_aliased, x_ref, wa_ref, wf_ref, wb_ref, wo_ref, bf_ref, qg_ref, kg_ref, cw_ref, ag_ref, cg_ref,
                   up_ref, *refs):
    (y_ref, ko_ref, vo_ref, fo_ref, co_ref,
     kn_ref, vt_ref, ck_ref, qm_ref, z_ref, m_ref, al_ref, acc_ref, carry_ref, uprev_ref,
     gate_ref, zt_ref) = refs[n_aliased:]
    i = pl.program_id(1)
    nt = pl.num_programs(1)
    t = SEQ_TILE
    tk = KEY_TILE
    n_sub = t // tk
    base = n_sub * i

    @pl.when(i == 0)
    def _():
        carry_ref[...] = jnp.zeros_like(carry_ref)
        uprev_ref[...] = jnp.zeros_like(uprev_ref)

    x = x_ref[0]
    ht = x.astype(BF16).T
    ms = jnp.mean(x * x, axis=-1, keepdims=True) + EPS
    ms_row = jnp.broadcast_to(ms, (t, LANES)).T[:1]
    r_row = lax.rsqrt(ms_row)
    eps_row = EPS * ms_row

    def proj(w_ref, off, extra=None):
        half = ATT_DIM // 2
        second = w_ref[0, off + half:off + ATT_DIM, :]
        if extra is not None:
            second = jnp.concatenate([second, extra], axis=0)
        return jnp.concatenate([_dot(w_ref[0, off:off + half, :], ht), _dot(second, ht)], axis=0)

    def out_proj(a, row0):
        half = D_MODEL // 2
        return jnp.concatenate([_dot(a, wo_ref[0, row0:row0 + ATT_DIM, :half]),
                                _dot(a, wo_ref[0, row0:row0 + ATT_DIM, half:])], axis=1)

    kf = proj(wa_ref, OFF_K, wf_ref[0])
    logf = _log_sigmoid(kf[ATT_DIM:] * r_row + _lanes(bf_ref[...], t))
    fo_ref[0, 0] = logf[:N_HEADS]

    kt = _head_rms_t(kf[:ATT_DIM], kg_ref[...], eps_row)
    ko_ref[0, 0] = kt.reshape(N_HEADS, HEAD_DIM, t)
    kn = kt.astype(BF16).T
    for s in range(n_sub):
        kn_ref[base + s] = kn[s * tk:(s + 1) * tk]
    vt = proj(wa_ref, OFF_V) * r_row
    vo_ref[0, 0] = vt.reshape(N_HEADS, HEAD_DIM, t)
    ones_rows = jnp.ones((PV_ROWS - HEAD_DIM, tk), BF16)
    for hd in range(N_HEADS):
        v_h = vt[hd * HEAD_DIM:(hd + 1) * HEAD_DIM].astype(BF16)
        for s in range(n_sub):
            vt_ref[hd, base + s] = jnp.concatenate([v_h[:, s * tk:(s + 1) * tk], ones_rows], axis=0)

    qt = (_head_rms_t(proj(wa_ref, OFF_Q), qg_ref[...], eps_row) * (ATTN_SCALE * LOG2E)).astype(BF16)
    zero_rows = jnp.zeros((HEAD_DIM, t), BF16)
    for hd in range(N_HEADS):
        q_h = qt[hd * HEAD_DIM:(hd + 1) * HEAD_DIM]
        qm_ref[hd] = jnp.concatenate([q_h, zero_rows] if hd % 2 == 0 else [zero_rows, q_h], axis=0)

    c3 = _dot(jnp.concatenate(_split3(logf), axis=0), up_ref[...])
    c = (c3[:F_ROWS] + c3[F_ROWS:2 * F_ROWS] + c3[2 * F_ROWS:])[:N_HEADS] + _lanes(carry_ref[...], t)
    carry_ref[...] = jnp.broadcast_to(c[:, t - 1:], carry_ref.shape)
    c_pos = jnp.concatenate([c, jnp.zeros((LANES - N_HEADS, t), F32)], axis=0).T
    for hd in range(N_HEADS):
        c_rep = jnp.broadcast_to(c_pos[:, hd:hd + 1] * LOG2E, (t, LANES))
        for s in range(n_sub):
            ck_ref[hd, base + s] = c_rep[s * tk:(s + 1) * tk]

    m_ref[...] = jnp.full(m_ref.shape, NEG, F32)
    acc_ref[...] = jnp.zeros_like(acc_ref)

    def track_max(hd, z, c0):
        z_ref[hd, :, c0:] = z
        m_old = m_ref[hd, :, c0:]
        m_new = jnp.maximum(m_old, jnp.max(z, axis=0, keepdims=True))
        al_ref[hd, :, c0:] = jnp.exp2(m_old - m_new)
        m_ref[hd, :, c0:] = m_new

    def scores_full(hd, j):
        p = hd // 2
        track_max(hd, _dot(kn_ref[j, :, p * LANES:(p + 1) * LANES], qm_ref[hd]) - _lanes(ck_ref[hd, j], t), 0)

    def scores_own(hd, s):
        p = hd // 2
        c0 = s * tk
        z = (_dot(kn_ref[base + s, :, p * LANES:(p + 1) * LANES], qm_ref[hd, :, c0:])
             - _lanes(ck_ref[hd, base + s], t - c0))
        keep = lax.broadcasted_iota(jnp.int32, (tk, tk), 0) <= lax.broadcasted_iota(jnp.int32, (tk, tk), 1)
        diag = jnp.where(keep, z[:, :tk], NEG)
        z = diag if t - c0 == tk else jnp.concatenate([diag, z[:, tk:]], axis=1)
        track_max(hd, z, c0)

    def accumulate(hd, j, c0=0):
        pr = jnp.exp2(z_ref[hd, :, c0:] - m_ref[hd, :, c0:]).astype(BF16)
        acc_ref[hd, :, c0:] = al_ref[hd, :, c0:] * acc_ref[hd, :, c0:] + _dot(vt_ref[hd, j], pr)

    ut = proj(wb_ref, OFF_CC) * proj(wb_ref, OFF_CH) * (r_row * r_row)
    prev = uprev_ref[...]
    yc = (_lanes(cw_ref[0], t) * _shift_lanes(ut, prev, 2) + _lanes(cw_ref[1], t) * _shift_lanes(ut, prev, 1)
          + _lanes(cw_ref[2], t) * ut)
    zn = _rms_t(proj(wb_ref, OFF_CB) * r_row * yc, cg_ref[...]) * _silu(proj(wb_ref, OFF_GC) * r_row)
    uprev_ref[...] = ut[:, t - LANES:]
    zt_ref[...] = zn.astype(BF16).T
    gate_ref[...] = _silu(proj(wa_ref, OFF_GA) * r_row)

    for hd in range(N_HEADS):
        scores_own(hd, n_sub - 1)
    for s in range(n_sub - 2, -1, -1):
        for hd in range(N_HEADS):
            accumulate(hd, base + s + 1, (s + 1) * tk)
            scores_own(hd, s)

    def body(g, carry):
        first = n_sub * g
        prev = jnp.where(g == 0, base, first - 1)
        for s in range(n_sub):
            for hd in range(N_HEADS):
                accumulate(hd, prev if s == 0 else first + s - 1)
                scores_full(hd, first + s)
        return carry

    lax.fori_loop(0, i, body, 0)
    last = jnp.where(i == 0, base, base - 1)
    quarter = D_MODEL // 4
    for hd in range(N_HEADS):
        accumulate(hd, last)
        if hd % 2 == 1:
            cs = slice((hd // 2) * quarter, (hd // 2 + 1) * quarter)
            y_ref[0, :, cs] = x_ref[0, :, cs] + _dot(zt_ref[...], wo_ref[0, ATT_DIM:, cs])

    outs = []
    for hd in range(N_HEADS):
        a = acc_ref[hd]
        outs.append(a[:HEAD_DIM] / a[HEAD_DIM:HEAD_DIM + 1])
    att = jnp.concatenate(outs, axis=0)
    an = (_rms_t(att, ag_ref[...]) * gate_ref[...]).astype(BF16).T
    y_ref[0] = y_ref[0] + out_proj(an, 0)

    @pl.when(i == nt - 1)
    def _():
        co_ref[0] = uprev_ref[...].T[LANES - (CONV_W - 1):, :]


def _const_spec(shape):
    nd = len(shape)
    return pl.BlockSpec(shape, lambda *_: (0,) * nd)


def _layer_spec(a, layer):
    nd = a.ndim - 1
    return pl.BlockSpec((1,) + a.shape[1:], lambda *_: (layer,) + (0,) * nd)


def _prompt_layer(layer, depth, x, weights, consts, stacked):
    b, s, d = x.shape
    t = SEQ_TILE
    tk = KEY_TILE
    nt = s // t
    nk = s // tk
    any_spec = pl.BlockSpec(memory_space=pl.ANY)
    n_in = 1 + len(weights) + len(consts)
    return pl.pallas_call(
        functools.partial(_prompt_kernel, len(stacked)),
        grid=(b, nt),
        in_specs=[pl.BlockSpec((1, t, d), lambda bi, ti: (bi, ti, 0))] + [_layer_spec(a, layer) for a in weights]
                 + [_const_spec(a.shape) for a in consts] + [any_spec] * len(stacked),
        out_specs=[pl.BlockSpec((1, t, d), lambda bi, ti: (bi, ti, 0)),
                   pl.BlockSpec((1, 1, N_HEADS, HEAD_DIM, t), lambda bi, ti: (layer, bi, 0, 0, ti)),
                   pl.BlockSpec((1, 1, N_HEADS, HEAD_DIM, t), lambda bi, ti: (layer, bi, 0, 0, ti)),
                   pl.BlockSpec((1, 1, N_HEADS, t), lambda bi, ti: (layer, bi, 0, ti)),
                   pl.BlockSpec((1, CONV_W - 1, CONV_DIM), lambda bi, ti: (bi, 0, 0))],
        out_shape=[jax.ShapeDtypeStruct((b, s, d), F32),
                   jax.ShapeDtypeStruct((depth, b, N_HEADS, HEAD_DIM, s), F32),
                   jax.ShapeDtypeStruct((depth, b, N_HEADS, HEAD_DIM, s), F32),
                   jax.ShapeDtypeStruct((depth, b, N_HEADS, s), F32),
                   jax.ShapeDtypeStruct((b, CONV_W - 1, CONV_DIM), F32)],
        input_output_aliases={n_in + a: 1 + a for a in range(len(stacked))},
        scratch_shapes=[
            pltpu.VMEM((nk, tk, ATT_DIM), BF16),
            pltpu.VMEM((N_HEADS, nk, PV_ROWS, tk), BF16),
            pltpu.VMEM((N_HEADS, nk, tk, LANES), F32),
            pltpu.VMEM((N_HEADS, LANES, t), BF16),
            pltpu.VMEM((N_HEADS, tk, t), F32),
            pltpu.VMEM((N_HEADS, 1, t), F32),
            pltpu.VMEM((N_HEADS, 1, t), F32),
            pltpu.VMEM((N_HEADS, PV_ROWS, t), F32),
            pltpu.VMEM((N_HEADS, LANES), F32),
            pltpu.VMEM((CONV_DIM, LANES), F32),
            pltpu.VMEM((ATT_DIM, t), F32),
            pltpu.VMEM((t, CONV_DIM), BF16),
        ],
        compiler_params=pltpu.CompilerParams(
            dimension_semantics=("arbitrary", "arbitrary"), vmem_limit_bytes=VMEM_LIMIT),
        name="prompt_layer",
    )(x, *weights, *consts, *stacked)


def _rms(x, g):
    ms = jnp.mean(x * x, axis=-1, keepdims=True)
    return x * lax.rsqrt(ms + EPS) * g


def _head_rms(x, g_tiled, bd):
    ss = _dot((x * x).astype(BF16), bd)
    return x * lax.rsqrt(ss * (1.0 / HEAD_DIM) + EPS) * g_tiled


def _shift_rows(u, prev8, k):
    r = pltpu.roll(u, k, axis=0)
    pr = pltpu.roll(prev8, k, axis=0)
    rowid = lax.broadcasted_iota(jnp.int32, prev8.shape, 0)
    head = jnp.where(rowid < k, pr, r[0:SUBLANES])
    return jnp.concatenate([head, r[SUBLANES:]], axis=0)


def _sample_kernel(x_ref, pk_ref, pv_ref, pf_ref, st_ref,
                   wa_ref, wf_ref, wb_ref, wo_ref, bf_ref, qg_ref, kg_ref, cw_ref, ag_ref, cg_ref,
                   bd_ref, tri_ref, upper_ref, ones_ref, blk_ref,
                   y_ref, ko_ref, vo_ref, fo_ref, co_ref):
    n = x_ref.shape[1]
    n_blk = pf_ref.shape[1] // N_HEADS
    past = pk_ref.shape[-1]
    x = x_ref[0]
    h = _rms(x, 1.0).astype(BF16)
    bd = bd_ref[...]
    q = _head_rms(_dot(h, wa_ref[0, :, OFF_Q:OFF_Q + ATT_DIM]), qg_ref[...], bd) * (ATTN_SCALE * LOG2E)
    k = _head_rms(_dot(h, wa_ref[0, :, OFF_K:OFF_K + ATT_DIM]), kg_ref[...], bd)
    v = _dot(h, wa_ref[0, :, OFF_V:OFF_V + ATT_DIM])
    logf = _log_sigmoid(_dot(h, wf_ref[0]) + bf_ref[...])
    ko_ref[0] = k
    vo_ref[0] = v
    fo_ref[0] = logf[:, :N_HEADS]

    pf = pf_ref[0]
    tot = _dot_exact_lhs(pf, ones_ref[...])
    before = _dot_exact_rhs(blk_ref[...], tot)
    c_past = _dot_exact_lhs(pf, upper_ref[...]) + before
    ck_past = jnp.concatenate(
        [c_past[bi * N_HEADS:(bi + 1) * N_HEADS] for bi in range(n_blk)], axis=1) * LOG2E
    past_total = (before + tot)[(n_blk - 1) * N_HEADS:, :LANES]

    zpad = jnp.zeros((LANES - n, F_PAD), F32)
    c_new = _dot_exact_rhs(tri_ref[...], jnp.concatenate([logf, zpad], axis=0))
    ck_new = (c_new.T[:N_HEADS, :] + past_total) * LOG2E

    row = lax.broadcasted_iota(jnp.int32, (n, LANES), 0)
    col = lax.broadcasted_iota(jnp.int32, (n, LANES), 1)
    keep = col <= row
    low = lax.broadcasted_iota(jnp.int32, (n, LANES), 1) < HEAD_DIM
    cols = []
    for p in range(N_PAIRS):
        sl = slice(p * LANES, (p + 1) * LANES)
        kpt = pk_ref[0, 0, 2 * p:2 * p + 2].reshape(LANES, past).astype(BF16)
        vpt = pv_ref[0, 0, 2 * p:2 * p + 2].reshape(LANES, past).astype(BF16)
        zrows = jnp.zeros((LANES - n, LANES), F32)
        kn = jnp.concatenate([k[:, sl], zrows], axis=0).astype(BF16)
        vn = jnp.concatenate([v[:, sl], zrows], axis=0).astype(BF16)
        o = []
        for hd in (2 * p, 2 * p + 1):
            keep_q = low if hd % 2 == 0 else jnp.logical_not(low)
            qm = jnp.where(keep_q, q[:, sl], 0.0).astype(BF16)
            z_past = _dot(qm, kpt) - ck_past[hd:hd + 1, :]
            z_new = jnp.where(keep, _dot_nt(qm, kn) - ck_new[hd:hd + 1, :], NEG)
            m = jnp.maximum(jnp.max(z_past, axis=-1, keepdims=True), jnp.max(z_new, axis=-1, keepdims=True))
            p_past = jnp.exp2(z_past - m)
            p_new = jnp.exp2(z_new - m)
            l = jnp.sum(p_past, axis=-1, keepdims=True) + jnp.sum(p_new, axis=-1, keepdims=True)
            o.append((_dot_nt(p_past.astype(BF16), vpt) + _dot(p_new.astype(BF16), vn)) / l)
        cols.append(jnp.where(low, o[0], o[1]))
    att = jnp.concatenate(cols, axis=1)

    st = st_ref[0, 0]
    u_prev8 = jnp.concatenate([jnp.zeros((SUBLANES - (CONV_W - 1), CONV_DIM), F32), st], axis=0)
    cc = _dot(h, wb_ref[0, :, OFF_CC:OFF_CC + CONV_DIM])
    ch = _dot(h, wb_ref[0, :, OFF_CH:OFF_CH + CONV_DIM])
    u = cc * ch
    cw = cw_ref[...]
    yc = cw[0:1] * _shift_rows(u, u_prev8, 2) + cw[1:2] * _shift_rows(u, u_prev8, 1) + cw[2:3] * u
    z = _dot(h, wb_ref[0, :, OFF_CB:OFF_CB + CONV_DIM]) * yc
    zn = _rms(z, cg_ref[...]) * _silu(_dot(h, wb_ref[0, :, OFF_GC:OFF_GC + CONV_DIM]))
    an = _rms(att, ag_ref[...]) * _silu(_dot(h, wa_ref[0, :, OFF_GA:OFF_GA + ATT_DIM]))
    mix = jnp.concatenate([an, zn], axis=1).astype(BF16)
    y_ref[0] = x + _dot(mix, wo_ref[0])
    co_ref[0] = u[n - (CONV_W - 1):, :]


def _sample_layer(layer, x, pk_t, pv_t, pf, st_all, weights, consts):
    b, n, d = x.shape
    past = pk_t.shape[-1]
    per_b = lambda a: pl.BlockSpec((1,) + a.shape[1:], lambda bi: (bi,) + (0,) * (a.ndim - 1))
    per_lb = lambda a: pl.BlockSpec((1, 1) + a.shape[2:], lambda bi: (layer, bi) + (0,) * (a.ndim - 2))
    out_shape = [jax.ShapeDtypeStruct((b, n, d), F32),
                 jax.ShapeDtypeStruct((b, n, ATT_DIM), F32),
                 jax.ShapeDtypeStruct((b, n, ATT_DIM), F32),
                 jax.ShapeDtypeStruct((b, n, N_HEADS), F32),
                 jax.ShapeDtypeStruct((b, CONV_W - 1, CONV_DIM), F32)]
    return pl.pallas_call(
        _sample_kernel,
        grid=(b,),
        in_specs=[per_b(x), per_lb(pk_t), per_lb(pv_t), per_b(pf), per_lb(st_all)]
                 + [_layer_spec(a, layer) for a in weights] + [_const_spec(a.shape) for a in consts],
        out_specs=[per_b(a) for a in out_shape],
        out_shape=out_shape,
        compiler_params=pltpu.CompilerParams(
            dimension_semantics=("arbitrary",), vmem_limit_bytes=VMEM_LIMIT),
        name="sample_layer",
    )(x, pk_t, pv_t, pf, st_all, *weights, *consts)


PREP_ROWS = 256


def _prep_kernel(w_ref, g_ref, o_ref, wa_ref, wf_ref, wb_ref, na_ref, nf_ref, nb_ref, wo_ref):
    g = g_ref[0]
    f0 = 4 * ATT_DIM
    for r0 in range(0, f0, PREP_ROWS):
        a = (w_ref[0, r0:r0 + PREP_ROWS, :] * g).astype(BF16)
        b = (w_ref[0, f0 + N_HEADS + r0:f0 + N_HEADS + r0 + PREP_ROWS, :] * g).astype(BF16)
        wa_ref[0, r0:r0 + PREP_ROWS, :] = a
        wb_ref[0, r0:r0 + PREP_ROWS, :] = b
        na_ref[0, :, r0:r0 + PREP_ROWS] = a.T
        nb_ref[0, :, r0:r0 + PREP_ROWS] = b.T
    wf = w_ref[0, f0:f0 + N_HEADS, :] * g
    wf = jnp.concatenate([wf, jnp.zeros((F_PAD - N_HEADS, D_MODEL), F32)], axis=0).astype(BF16)
    wf_ref[0] = wf[:F_ROWS]
    nf_ref[0] = wf.T
    for r0 in range(0, D_MODEL, PREP_ROWS):
        wo_ref[0, r0:r0 + PREP_ROWS, :] = o_ref[0, r0:r0 + PREP_ROWS, :].astype(BF16)


def _prep_weights(w_t, g, w_o):
    depth, n_in, d = w_t.shape
    n_half = 4 * ATT_DIM
    shapes = [(n_half, d), (F_ROWS, d), (n_half, d), (d, n_half), (d, F_PAD), (d, n_half), w_o.shape[1:]]
    per_l = lambda shape: pl.BlockSpec((1,) + shape, lambda l: (l, 0, 0), pipeline_mode=pl.Buffered(1))
    return pl.pallas_call(
        _prep_kernel,
        grid=(depth,),
        in_specs=[per_l((n_in, d)), per_l((1, d)), per_l(w_o.shape[1:])],
        out_specs=[per_l(sh) for sh in shapes],
        out_shape=[jax.ShapeDtypeStruct((depth,) + sh, BF16) for sh in shapes],
        compiler_params=pltpu.CompilerParams(
            dimension_semantics=("arbitrary",), vmem_limit_bytes=VMEM_LIMIT),
        name="prep_weights",
    )(w_t, g, w_o)


def _rep(v):
    return jnp.broadcast_to(v[..., None], v.shape + (LANES,))


def kernel(x_prompt, x_sample, cache_k, cache_v, cache_logf, state_conv, norm_g, w_in, b_f,
           q_norm_g, k_norm_g, conv_w, att_out_g, conv_out_g, w_out):
    depth = w_in.shape[0]
    bp, sp, _ = x_prompt.shape
    bs, ns, _ = x_sample.shape
    past = cache_k.shape[2]
    n_blk = past // PAST_BLK

    bd = jnp.asarray(np.kron(np.eye(N_HEADS, dtype=np.float32), np.ones((HEAD_DIM, HEAD_DIM), np.float32)), BF16)
    up_t = jnp.asarray(np.triu(np.ones((SEQ_TILE, SEQ_TILE), np.float32)), BF16)
    tri_l = jnp.asarray(np.tril(np.ones((LANES, LANES), np.float32)), BF16)
    upper = jnp.asarray(np.triu(np.ones((PAST_BLK, PAST_BLK), np.float32)), BF16)
    ones = jnp.ones((PAST_BLK, PAST_BLK), BF16)
    r = np.arange(n_blk * N_HEADS)
    blk = jnp.asarray(((r[:, None] % N_HEADS == r[None, :] % N_HEADS)
                       & (r[None, :] // N_HEADS < r[:, None] // N_HEADS)).astype(np.float32), BF16)

    pk_t = jnp.transpose(cache_k, (0, 1, 3, 4, 2))
    pv_t = jnp.transpose(cache_v, (0, 1, 3, 4, 2))

    *weights, wo_all = _prep_weights(jnp.transpose(w_in, (0, 2, 1)), norm_g[:, None, :], w_out)
    stacked = ()
    yp, ys = x_prompt, x_sample
    outs = [[] for _ in range(5)]
    for l in range(depth):
        bf_rows = _rep(jnp.concatenate([b_f[l], jnp.zeros((F_ROWS - N_HEADS,), F32)]))

        p_consts = (bf_rows, _rep(q_norm_g[l]), _rep(k_norm_g[l]), _rep(conv_w[l]),
                    _rep(att_out_g[l]), _rep(conv_out_g[l]), up_t)
        yp, *stacked, c1 = _prompt_layer(l, depth, yp, weights[:3] + [wo_all], p_consts, tuple(stacked))

        bfp = jnp.concatenate([b_f[l], jnp.zeros((F_PAD - N_HEADS,), F32)])[None, :]
        s_consts = (bfp, jnp.tile(q_norm_g[l], N_HEADS)[None, :], jnp.tile(k_norm_g[l], N_HEADS)[None, :],
                    conv_w[l], att_out_g[l][None, :], conv_out_g[l][None, :], bd, tri_l, upper, ones, blk)
        pf = cache_logf[l].reshape(bs, n_blk, PAST_BLK, N_HEADS).transpose(0, 1, 3, 2)
        pf = pf.reshape(bs, n_blk * N_HEADS, PAST_BLK)
        ys, k2, v2, f2, c2 = _sample_layer(l, ys, pk_t, pv_t, pf, state_conv, weights[3:] + [wo_all], s_consts)
        for lst, a in zip(outs, (c1, k2.reshape(bs, ns, N_HEADS, HEAD_DIM), v2.reshape(bs, ns, N_HEADS, HEAD_DIM),
                                 f2, c2)):
            lst.append(a)
    k_all, v_all, f_all = stacked
    k_prompt = jnp.transpose(k_all, (0, 1, 4, 2, 3))
    v_prompt = jnp.transpose(v_all, (0, 1, 4, 2, 3))
    f_prompt = jnp.transpose(f_all, (0, 1, 3, 2))
    c_prompt, k_s, v_s, f_s, c_s = (jnp.stack(o) for o in outs)
    return (yp, ys, k_prompt, v_prompt, f_prompt, c_prompt, k_s, v_s, f_s, c_s)
```

```python
import functools

import numpy as np
import jax
import jax.numpy as jnp
from jax import lax
from jax.experimental import pallas as pl
from jax.experimental.pallas import tpu as pltpu

D_MODEL = 1024
HEAD_DIM = 64
N_HEADS = 8
N_PAIRS = N_HEADS // 2
ATT_DIM = N_HEADS * HEAD_DIM
CONV_DIM = 512
CONV_W = 3
EPS = 1e-6
NEG = -1e30
ATTN_SCALE = HEAD_DIM ** -0.5
LOG2E = 1.4426950408889634

LANES = 128
SUBLANES = 8
BF16_ROWS = 16
F_PAD = LANES
F_ROWS = BF16_ROWS
OFF_Q, OFF_K, OFF_V, OFF_GA = 0, 512, 1024, 1536
OFF_CB, OFF_CC, OFF_CH, OFF_GC = 0, 512, 1024, 1536

SEQ_TILE = 512
KEY_TILE = 256
PV_ROWS = HEAD_DIM + BF16_ROWS
PAST_BLK = 256
VMEM_LIMIT = 58 * 1024 * 1024

BF16 = jnp.bfloat16
F32 = jnp.float32


def _dot(a, b):
    return jnp.dot(a, b, preferred_element_type=F32)


def _dot_nt(a, b):
    return lax.dot_general(a, b, (((1,), (1,)), ((), ())), preferred_element_type=F32)


def _split3(x):
    hi = x.astype(BF16)
    r = x - hi.astype(F32)
    mid = r.astype(BF16)
    lo = (r - mid.astype(F32)).astype(BF16)
    return hi, mid, lo


def _dot_exact_rhs(m01, x):
    hi, mid, lo = _split3(x)
    return _dot(m01, hi) + _dot(m01, mid) + _dot(m01, lo)


def _dot_exact_lhs(x, m01):
    hi, mid, lo = _split3(x)
    return _dot(hi, m01) + _dot(mid, m01) + _dot(lo, m01)


def _silu(x):
    return x / (1.0 + jnp.exp(-x))


def _log_sigmoid(x):
    return jnp.minimum(x, 0.0) - jnp.log1p(jnp.exp(-jnp.abs(x)))


def _lanes(rep, t):
    return jnp.concatenate([rep] * (t // LANES), axis=-1)


def _rms_t(xt, g_rep):
    t = xt.shape[1]
    ms = jnp.mean(xt * xt, axis=0, keepdims=True)
    return xt * lax.rsqrt(ms + EPS) * _lanes(g_rep, t)


def _head_rms_t(xt, g_rep, eps_row):
    t = xt.shape[1]
    x3 = xt.reshape(N_HEADS, HEAD_DIM, t)
    ms = jnp.mean(x3 * x3, axis=1, keepdims=True)
    return (x3 * lax.rsqrt(ms + eps_row[None]) * _lanes(g_rep, t)[None]).reshape(ATT_DIM, t)


def _shift_lanes(ut, prev, k):
    r = pltpu.roll(ut, k, axis=1)
    pr = pltpu.roll(prev, k, axis=1)
    lane = lax.broadcasted_iota(jnp.int32, prev.shape, 1)
    head = jnp.where(lane < k, pr, r[:, :LANES])
    return jnp.concatenate([head, r[:, LANES:]], axis=1)


def _prompt_kernel(n_aliased, x_ref, wa_ref, wf_ref, wb_ref, wo_ref, bf_ref, qg_ref, kg_ref, cw_ref, ag_ref, cg_ref,
                   up_ref, *refs):
    (y_ref, ko_ref, vo_ref, fo_ref, co_ref,
     kn_ref, vt_ref, ck_ref, qm_ref, z_ref, m_ref, al_ref, acc_ref, carry_ref, uprev_ref,
     gate_ref, zt_ref) = refs[n_aliased:]
    i = pl.program_id(1)
    nt = pl.num_programs(1)
    t = SEQ_TILE
    tk = KEY_TILE
    n_sub = t // tk
    base = n_sub * i

    @pl.when(i == 0)
    def _():
        carry_ref[...] = jnp.zeros_like(carry_ref)
        uprev_ref[...] = jnp.zeros_like(uprev_ref)

    x = x_ref[0]
    ht = x.astype(BF16).T
    ms = jnp.mean(x * x, axis=-1, keepdims=True) + EPS
    ms_row = jnp.broadcast_to(ms, (t, LANES)).T[:1]
    r_row = lax.rsqrt(ms_row)
    eps_row = EPS * ms_row

    def proj(w_ref, off, extra=None):
        half = ATT_DIM // 2
        second = w_ref[0, off + half:off + ATT_DIM, :]
        if extra is not None:
            second = jnp.concatenate([second, extra], axis=0)
        return jnp.concatenate([_dot(w_ref[0, off:off + half, :], ht), _dot(second, ht)], axis=0)

    def out_proj(a, row0):
        half = D_MODEL // 2
        return jnp.concatenate([_dot(a, wo_ref[0, row0:row0 + ATT_DIM, :half]),
                                _dot(a, wo_ref[0, row0:row0 + ATT_DIM, half:])], axis=1)

    kf = proj(wa_ref, OFF_K, wf_ref[0])
    logf = _log_sigmoid(kf[ATT_DIM:] * r_row + _lanes(bf_ref[...], t))
    fo_ref[0, 0] = logf[:N_HEADS]

    kt = _head_rms_t(kf[:ATT_DIM], kg_ref[...], eps_row)
    ko_ref[0, 0] = kt.reshape(N_HEADS, HEAD_DIM, t)
    kn = kt.astype(BF16).T
    for s in range(n_sub):
        kn_ref[base + s] = kn[s * tk:(s + 1) * tk]
    vt = proj(wa_ref, OFF_V) * r_row
    vo_ref[0, 0] = vt.reshape(N_HEADS, HEAD_DIM, t)
    ones_rows = jnp.ones((PV_ROWS - HEAD_DIM, tk), BF16)
    for hd in range(N_HEADS):
        v_h = vt[hd * HEAD_DIM:(hd + 1) * HEAD_DIM].astype(BF16)
        for s in range(n_sub):
            vt_ref[hd, base + s] = jnp.concatenate([v_h[:, s * tk:(s + 1) * tk], ones_rows], axis=0)

    qt = (_head_rms_t(proj(wa_ref, OFF_Q), qg_ref[...], eps_row) * (ATTN_SCALE * LOG2E)).astype(BF16)
    zero_rows = jnp.zeros((HEAD_DIM, t), BF16)
    for hd in range(N_HEADS):
        q_h = qt[hd * HEAD_DIM:(hd + 1) * HEAD_DIM]
        qm_ref[hd] = jnp.concatenate([q_h, zero_rows] if hd % 2 == 0 else [zero_rows, q_h], axis=0)

    c3 = _dot(jnp.concatenate(_split3(logf), axis=0), up_ref[...])
    c = (c3[:F_ROWS] + c3[F_ROWS:2 * F_ROWS] + c3[2 * F_ROWS:])[:N_HEADS] + _lanes(carry_ref[...], t)
    carry_ref[...] = jnp.broadcast_to(c[:, t - 1:], carry_ref.shape)
    c_pos = jnp.concatenate([c, jnp.zeros((LANES - N_HEADS, t), F32)], axis=0).T
    for hd in range(N_HEADS):
        c_rep = jnp.broadcast_to(c_pos[:, hd:hd + 1] * LOG2E, (t, LANES))
        for s in range(n_sub):
            ck_ref[hd, base + s] = c_rep[s * tk:(s + 1) * tk]

    m_ref[...] = jnp.full(m_ref.shape, NEG, F32)
    acc_ref[...] = jnp.zeros_like(acc_ref)

    def track_max(hd, z, c0):
        z_ref[hd, :, c0:] = z
        m_old = m_ref[hd, :, c0:]
        m_new = jnp.maximum(m_old, jnp.max(z, axis=0, keepdims=True))
        al_ref[hd, :, c0:] = jnp.exp2(m_old - m_new)
        m_ref[hd, :, c0:] = m_new

    def scores_full(hd, j):
        p = hd // 2
        track_max(hd, _dot(kn_ref[j, :, p * LANES:(p + 1) * LANES], qm_ref[hd]) - _lanes(ck_ref[hd, j], t), 0)

    def scores_own(hd, s):
        p = hd // 2
        c0 = s * tk
        z = (_dot(kn_ref[base + s, :, p * LANES:(p + 1) * LANES], qm_ref[hd, :, c0:])
             - _lanes(ck_ref[hd, base + s], t - c0))
        keep = lax.broadcasted_iota(jnp.int32, (tk, tk), 0) <= lax.broadcasted_iota(jnp.int32, (tk, tk), 1)
        diag = jnp.where(keep, z[:, :tk], NEG)
        z = diag if t - c0 == tk else jnp.concatenate([diag, z[:, tk:]], axis=1)
        track_max(hd, z, c0)

    def accumulate(hd, j, c0=0):
        pr = jnp.exp2(z_ref[hd, :, c0:] - m_ref[hd, :, c0:]).astype(BF16)
        acc_ref[hd, :, c0:] = al_ref[hd, :, c0:] * acc_ref[hd, :, c0:] + _dot(vt_ref[hd, j], pr)

    ut = proj(wb_ref, OFF_CC) * proj(wb_ref, OFF_CH) * (r_row * r_row)
    prev = uprev_ref[...]
    yc = (_lanes(cw_ref[0], t) * _shift_lanes(ut, prev, 2) + _lanes(cw_ref[1], t) * _shift_lanes(ut, prev, 1)
          + _lanes(cw_ref[2], t) * ut)
    zn = _rms_t(proj(wb_ref, OFF_CB) * r_row * yc, cg_ref[...]) * _silu(proj(wb_ref, OFF_GC) * r_row)
    uprev_ref[...] = ut[:, t - LANES:]
    zt_ref[...] = zn.astype(BF16).T
    gate_ref[...] = _silu(proj(wa_ref, OFF_GA) * r_row)

    for hd in range(N_HEADS):
        scores_own(hd, n_sub - 1)
    for s in range(n_sub - 2, -1, -1):
        for hd in range(N_HEADS):
            accumulate(hd, base + s + 1, (s + 1) * tk)
            scores_own(hd, s)

    def body(g, carry):
        first = n_sub * g
        prev = jnp.where(g == 0, base, first - 1)
        for s in range(n_sub):
            for hd in range(N_HEADS):
                accumulate(hd, prev if s == 0 else first + s - 1)
                scores_full(hd, first + s)
        return carry

    lax.fori_loop(0, i, body, 0)
    last = jnp.where(i == 0, base, base - 1)
    quarter = D_MODEL // 4
    for hd in range(N_HEADS):
        accumulate(hd, last)
        if hd % 2 == 1:
            cs = slice((hd // 2) * quarter, (hd // 2 + 1) * quarter)
            y_ref[0, :, cs] = x_ref[0, :, cs] + _dot(zt_ref[...], wo_ref[0, ATT_DIM:, cs])

    outs = []
    for hd in range(N_HEADS):
        a = acc_ref[hd]
        outs.append(a[:HEAD_DIM] / a[HEAD_DIM:HEAD_DIM + 1])
    att = jnp.concatenate(outs, axis=0)
    an = (_rms_t(att, ag_ref[...]) * gate_ref[...]).astype(BF16).T
    y_ref[0] = y_ref[0] + out_proj(an, 0)

    @pl.when(i == nt - 1)
    def _():
        co_ref[0, 0] = uprev_ref[...].T[LANES - (CONV_W - 1):, :]


def _const_spec(shape):
    nd = len(shape)
    return pl.BlockSpec(shape, lambda *_: (0,) * nd)


def _layer_spec(a, layer):
    nd = a.ndim - 1
    return pl.BlockSpec((1,) + a.shape[1:], lambda *_: (layer,) + (0,) * nd)


def _prompt_layer(layer, depth, x, weights, consts, stacked):
    b, s, d = x.shape
    t = SEQ_TILE
    tk = KEY_TILE
    nt = s // t
    nk = s // tk
    any_spec = pl.BlockSpec(memory_space=pl.ANY)
    n_in = 1 + len(weights) + len(consts)
    return pl.pallas_call(
        functools.partial(_prompt_kernel, len(stacked)),
        grid=(b, nt),
        in_specs=[pl.BlockSpec((1, t, d), lambda bi, ti: (bi, ti, 0))] + [_layer_spec(a, layer) for a in weights]
                 + [_const_spec(a.shape) for a in consts] + [any_spec] * len(stacked),
        out_specs=[pl.BlockSpec((1, t, d), lambda bi, ti: (bi, ti, 0)),
                   pl.BlockSpec((1, 1, N_HEADS, HEAD_DIM, t), lambda bi, ti: (layer, bi, 0, 0, ti)),
                   pl.BlockSpec((1, 1, N_HEADS, HEAD_DIM, t), lambda bi, ti: (layer, bi, 0, 0, ti)),
                   pl.BlockSpec((1, 1, N_HEADS, t), lambda bi, ti: (layer, bi, 0, ti)),
                   pl.BlockSpec((1, 1, CONV_W - 1, CONV_DIM), lambda bi, ti: (layer, bi, 0, 0))],
        out_shape=[jax.ShapeDtypeStruct((b, s, d), F32),
                   jax.ShapeDtypeStruct((depth, b, N_HEADS, HEAD_DIM, s), F32),
                   jax.ShapeDtypeStruct((depth, b, N_HEADS, HEAD_DIM, s), F32),
                   jax.ShapeDtypeStruct((depth, b, N_HEADS, s), F32),
                   jax.ShapeDtypeStruct((depth, b, CONV_W - 1, CONV_DIM), F32)],
        input_output_aliases={n_in + a: 1 + a for a in range(len(stacked))},
        scratch_shapes=[
            pltpu.VMEM((nk, tk, ATT_DIM), BF16),
            pltpu.VMEM((N_HEADS, nk, PV_ROWS, tk), BF16),
            pltpu.VMEM((N_HEADS, nk, tk, LANES), F32),
            pltpu.VMEM((N_HEADS, LANES, t), BF16),
            pltpu.VMEM((N_HEADS, tk, t), F32),
            pltpu.VMEM((N_HEADS, 1, t), F32),
            pltpu.VMEM((N_HEADS, 1, t), F32),
            pltpu.VMEM((N_HEADS, PV_ROWS, t), F32),
            pltpu.VMEM((N_HEADS, LANES), F32),
            pltpu.VMEM((CONV_DIM, LANES), F32),
            pltpu.VMEM((ATT_DIM, t), F32),
            pltpu.VMEM((t, CONV_DIM), BF16),
        ],
        compiler_params=pltpu.CompilerParams(
            dimension_semantics=("arbitrary", "arbitrary"), vmem_limit_bytes=VMEM_LIMIT),
        name="prompt_layer",
    )(x, *weights, *consts, *stacked)


def _rms(x, g):
    ms = jnp.mean(x * x, axis=-1, keepdims=True)
    return x * lax.rsqrt(ms + EPS) * g


def _head_rms(x, g_tiled, bd):
    ss = _dot((x * x).astype(BF16), bd)
    return x * lax.rsqrt(ss * (1.0 / HEAD_DIM) + EPS) * g_tiled


def _shift_rows(u, prev8, k):
    r = pltpu.roll(u, k, axis=0)
    pr = pltpu.roll(prev8, k, axis=0)
    rowid = lax.broadcasted_iota(jnp.int32, prev8.shape, 0)
    head = jnp.where(rowid < k, pr, r[0:SUBLANES])
    return jnp.concatenate([head, r[SUBLANES:]], axis=0)


def _sample_kernel(x_ref, pk_ref, pv_ref, pf_ref, st_ref,
                   wa_ref, wf_ref, wb_ref, wo_ref, bf_ref, qg_ref, kg_ref, cw_ref, ag_ref, cg_ref,
                   bd_ref, tri_ref, upper_ref, ones_ref, blk_ref,
                   *refs):
    y_ref, ko_ref, vo_ref, fo_ref, co_ref = refs[-5:]
    n = x_ref.shape[1]
    n_blk = pf_ref.shape[1] // N_HEADS
    past = pk_ref.shape[-1]
    x = x_ref[0]
    h = _rms(x, 1.0).astype(BF16)
    bd = bd_ref[...]
    q = _head_rms(_dot(h, wa_ref[0, :, OFF_Q:OFF_Q + ATT_DIM]), qg_ref[...], bd) * (ATTN_SCALE * LOG2E)
    k = _head_rms(_dot(h, wa_ref[0, :, OFF_K:OFF_K + ATT_DIM]), kg_ref[...], bd)
    v = _dot(h, wa_ref[0, :, OFF_V:OFF_V + ATT_DIM])
    logf = _log_sigmoid(_dot(h, wf_ref[0]) + bf_ref[...])
    ko_ref[0, 0] = k
    vo_ref[0, 0] = v
    fo_ref[0, 0] = logf[:, :N_HEADS]

    pf = pf_ref[0]
    tot = _dot_exact_lhs(pf, ones_ref[...])
    before = _dot_exact_rhs(blk_ref[...], tot)
    c_past = _dot_exact_lhs(pf, upper_ref[...]) + before
    ck_past = jnp.concatenate(
        [c_past[bi * N_HEADS:(bi + 1) * N_HEADS] for bi in range(n_blk)], axis=1) * LOG2E
    past_total = (before + tot)[(n_blk - 1) * N_HEADS:, :LANES]

    zpad = jnp.zeros((LANES - n, F_PAD), F32)
    c_new = _dot_exact_rhs(tri_ref[...], jnp.concatenate([logf, zpad], axis=0))
    ck_new = (c_new.T[:N_HEADS, :] + past_total) * LOG2E

    row = lax.broadcasted_iota(jnp.int32, (n, LANES), 0)
    col = lax.broadcasted_iota(jnp.int32, (n, LANES), 1)
    keep = jnp.concatenate([col <= row] * 2, axis=0)
    low = lax.broadcasted_iota(jnp.int32, (n, LANES), 1) < HEAD_DIM
    cols = []
    for p in range(N_PAIRS):
        sl = slice(p * LANES, (p + 1) * LANES)
        kpt = pk_ref[0, 0, 2 * p:2 * p + 2].reshape(LANES, past).astype(BF16)
        vpt = pv_ref[0, 0, 2 * p:2 * p + 2].reshape(LANES, past).astype(BF16)
        zrows = jnp.zeros((LANES - n, LANES), F32)
        kn = jnp.concatenate([k[:, sl], zrows], axis=0).astype(BF16)
        vn = jnp.concatenate([v[:, sl], zrows], axis=0).astype(BF16)
        q_p = q[:, sl]
        qm = jnp.concatenate([jnp.where(low, q_p, 0.0), jnp.where(low, 0.0, q_p)], axis=0).astype(BF16)
        hd = 2 * p
        c_past = jnp.concatenate([jnp.broadcast_to(ck_past[hd + e:hd + e + 1, :], (n, past)) for e in (0, 1)], axis=0)
        c_new = jnp.concatenate([jnp.broadcast_to(ck_new[hd + e:hd + e + 1, :], (n, LANES)) for e in (0, 1)], axis=0)
        z_past = _dot(qm, kpt) - c_past
        z_new = jnp.where(keep, _dot_nt(qm, kn) - c_new, NEG)
        m = jnp.maximum(jnp.max(z_past, axis=-1, keepdims=True), jnp.max(z_new, axis=-1, keepdims=True))
        p_past = jnp.exp2(z_past - m)
        p_new = jnp.exp2(z_new - m)
        l = jnp.sum(p_past, axis=-1, keepdims=True) + jnp.sum(p_new, axis=-1, keepdims=True)
        o = (_dot_nt(p_past.astype(BF16), vpt) + _dot(p_new.astype(BF16), vn)) / l
        cols.append(jnp.where(low, o[:n], o[n:]))
    att = jnp.concatenate(cols, axis=1)

    st = st_ref[0, 0]
    u_prev8 = jnp.concatenate([jnp.zeros((SUBLANES - (CONV_W - 1), CONV_DIM), F32), st], axis=0)
    cc = _dot(h, wb_ref[0, :, OFF_CC:OFF_CC + CONV_DIM])
    ch = _dot(h, wb_ref[0, :, OFF_CH:OFF_CH + CONV_DIM])
    u = cc * ch
    cw = cw_ref[...]
    yc = cw[0:1] * _shift_rows(u, u_prev8, 2) + cw[1:2] * _shift_rows(u, u_prev8, 1) + cw[2:3] * u
    z = _dot(h, wb_ref[0, :, OFF_CB:OFF_CB + CONV_DIM]) * yc
    zn = _rms(z, cg_ref[...]) * _silu(_dot(h, wb_ref[0, :, OFF_GC:OFF_GC + CONV_DIM]))
    an = _rms(att, ag_ref[...]) * _silu(_dot(h, wa_ref[0, :, OFF_GA:OFF_GA + ATT_DIM]))
    mix = jnp.concatenate([an, zn], axis=1).astype(BF16)
    y_ref[0] = x + _dot(mix, wo_ref[0])
    co_ref[0, 0] = u[n - (CONV_W - 1):, :]


def _sample_layer(layer, depth, x, pk_t, pv_t, pf, st_all, weights, consts, stacked):
    b, n, d = x.shape
    per_b = lambda a: pl.BlockSpec((1,) + a.shape[1:], lambda bi: (bi,) + (0,) * (a.ndim - 1))
    per_lb = lambda a: pl.BlockSpec((1, 1) + a.shape[2:], lambda bi: (layer, bi) + (0,) * (a.ndim - 2))
    y_shape = jax.ShapeDtypeStruct((b, n, d), F32)
    stacked_shapes = [jax.ShapeDtypeStruct((depth, b, n, ATT_DIM), F32),
                      jax.ShapeDtypeStruct((depth, b, n, ATT_DIM), F32),
                      jax.ShapeDtypeStruct((depth, b, n, N_HEADS), F32),
                      jax.ShapeDtypeStruct((depth, b, CONV_W - 1, CONV_DIM), F32)]
    n_in = 5 + len(weights) + len(consts)
    return pl.pallas_call(
        _sample_kernel,
        grid=(b,),
        in_specs=[per_b(x), per_lb(pk_t), per_lb(pv_t), per_b(pf), per_lb(st_all)]
                 + [_layer_spec(a, layer) for a in weights] + [_const_spec(a.shape) for a in consts]
                 + [pl.BlockSpec(memory_space=pl.ANY)] * len(stacked),
        out_specs=[per_b(y_shape)] + [per_lb(a) for a in stacked_shapes],
        out_shape=[y_shape] + stacked_shapes,
        input_output_aliases={n_in + a: 1 + a for a in range(len(stacked))},
        compiler_params=pltpu.CompilerParams(
            dimension_semantics=("arbitrary",), vmem_limit_bytes=VMEM_LIMIT),
        name="sample_layer",
    )(x, pk_t, pv_t, pf, st_all, *weights, *consts, *stacked)


PREP_ROWS = 256


def _prep_kernel(w_ref, g_ref, o_ref, wa_ref, wf_ref, wb_ref, na_ref, nf_ref, nb_ref, wo_ref):
    g = g_ref[0]
    f0 = 4 * ATT_DIM
    for r0 in range(0, f0, PREP_ROWS):
        a = (w_ref[0, r0:r0 + PREP_ROWS, :] * g).astype(BF16)
        b = (w_ref[0, f0 + N_HEADS + r0:f0 + N_HEADS + r0 + PREP_ROWS, :] * g).astype(BF16)
        wa_ref[0, r0:r0 + PREP_ROWS, :] = a
        wb_ref[0, r0:r0 + PREP_ROWS, :] = b
        na_ref[0, :, r0:r0 + PREP_ROWS] = a.T
        nb_ref[0, :, r0:r0 + PREP_ROWS] = b.T
    wf = w_ref[0, f0:f0 + N_HEADS, :] * g
    wf = jnp.concatenate([wf, jnp.zeros((F_PAD - N_HEADS, D_MODEL), F32)], axis=0).astype(BF16)
    wf_ref[0] = wf[:F_ROWS]
    nf_ref[0] = wf.T
    for r0 in range(0, D_MODEL, PREP_ROWS):
        wo_ref[0, r0:r0 + PREP_ROWS, :] = o_ref[0, r0:r0 + PREP_ROWS, :].astype(BF16)


def _prep_weights(w_t, g, w_o):
    depth, n_in, d = w_t.shape
    n_half = 4 * ATT_DIM
    shapes = [(n_half, d), (F_ROWS, d), (n_half, d), (d, n_half), (d, F_PAD), (d, n_half), w_o.shape[1:]]
    per_l = lambda shape: pl.BlockSpec((1,) + shape, lambda l: (l, 0, 0), pipeline_mode=pl.Buffered(1))
    return pl.pallas_call(
        _prep_kernel,
        grid=(depth,),
        in_specs=[per_l((n_in, d)), per_l((1, d)), per_l(w_o.shape[1:])],
        out_specs=[per_l(sh) for sh in shapes],
        out_shape=[jax.ShapeDtypeStruct((depth,) + sh, BF16) for sh in shapes],
        compiler_params=pltpu.CompilerParams(
            dimension_semantics=("arbitrary",), vmem_limit_bytes=VMEM_LIMIT),
        name="prep_weights",
    )(w_t, g, w_o)


def _rep(v):
    return jnp.broadcast_to(v[..., None], v.shape + (LANES,))


def kernel(x_prompt, x_sample, cache_k, cache_v, cache_logf, state_conv, norm_g, w_in, b_f,
           q_norm_g, k_norm_g, conv_w, att_out_g, conv_out_g, w_out):
    depth = w_in.shape[0]
    bp, sp, _ = x_prompt.shape
    bs, ns, _ = x_sample.shape
    past = cache_k.shape[2]
    n_blk = past // PAST_BLK

    bd = jnp.asarray(np.kron(np.eye(N_HEADS, dtype=np.float32), np.ones((HEAD_DIM, HEAD_DIM), np.float32)), BF16)
    up_t = jnp.asarray(np.triu(np.ones((SEQ_TILE, SEQ_TILE), np.float32)), BF16)
    tri_l = jnp.asarray(np.tril(np.ones((LANES, LANES), np.float32)), BF16)
    upper = jnp.asarray(np.triu(np.ones((PAST_BLK, PAST_BLK), np.float32)), BF16)
    ones = jnp.ones((PAST_BLK, PAST_BLK), BF16)
    r = np.arange(n_blk * N_HEADS)
    blk = jnp.asarray(((r[:, None] % N_HEADS == r[None, :] % N_HEADS)
                       & (r[None, :] // N_HEADS < r[:, None] // N_HEADS)).astype(np.float32), BF16)

    pk_t = jnp.transpose(cache_k, (0, 1, 3, 4, 2))
    pv_t = jnp.transpose(cache_v, (0, 1, 3, 4, 2))

    *weights, wo_all = _prep_weights(jnp.transpose(w_in, (0, 2, 1)), norm_g[:, None, :], w_out)
    stacked, stacked_s = (), ()
    yp, ys = x_prompt, x_sample
    for l in range(depth):
        bf_rows = _rep(jnp.concatenate([b_f[l], jnp.zeros((F_ROWS - N_HEADS,), F32)]))

        p_consts = (bf_rows, _rep(q_norm_g[l]), _rep(k_norm_g[l]), _rep(conv_w[l]),
                    _rep(att_out_g[l]), _rep(conv_out_g[l]), up_t)
        yp, *stacked = _prompt_layer(l, depth, yp, weights[:3] + [wo_all], p_consts, tuple(stacked))

        bfp = jnp.concatenate([b_f[l], jnp.zeros((F_PAD - N_HEADS,), F32)])[None, :]
        s_consts = (bfp, jnp.tile(q_norm_g[l], N_HEADS)[None, :], jnp.tile(k_norm_g[l], N_HEADS)[None, :],
                    conv_w[l], att_out_g[l][None, :], conv_out_g[l][None, :], bd, tri_l, upper, ones, blk)
        pf = cache_logf[l].reshape(bs, n_blk, PAST_BLK, N_HEADS).transpose(0, 1, 3, 2)
        pf = pf.reshape(bs, n_blk * N_HEADS, PAST_BLK)
        ys, *stacked_s = _sample_layer(l, depth, ys, pk_t, pv_t, pf, state_conv, weights[3:] + [wo_all], s_consts,
                                       tuple(stacked_s))
    k_all, v_all, f_all, c_prompt = stacked
    k_s, v_s, f_s, c_s = stacked_s
    k_prompt = jnp.transpose(k_all, (0, 1, 4, 2, 3))
    v_prompt = jnp.transpose(v_all, (0, 1, 4, 2, 3))
    f_prompt = jnp.transpose(f_all, (0, 1, 3, 2))
    k_s = k_s.reshape(depth, bs, ns, N_HEADS, HEAD_DIM)
    v_s = v_s.reshape(depth, bs, ns, N_HEADS, HEAD_DIM)
    return (yp, ys, k_prompt, v_prompt, f_prompt, c_prompt, k_s, v_s, f_s, c_s)
```

```python
import functools

import numpy as np
import jax
import jax.numpy as jnp
from jax import lax
from jax.experimental import pallas as pl
from jax.experimental.pallas import tpu as pltpu

D_MODEL = 1024
HEAD_DIM = 64
N_HEADS = 8
N_PAIRS = N_HEADS // 2
ATT_DIM = N_HEADS * HEAD_DIM
CONV_DIM = 512
CONV_W = 3
EPS = 1e-6
NEG = -1e30
ATTN_SCALE = HEAD_DIM ** -0.5
LOG2E = 1.4426950408889634

LANES = 128
SUBLANES = 8
BF16_ROWS = 16
F_PAD = LANES
F_ROWS = BF16_ROWS
OFF_Q, OFF_K, OFF_V, OFF_GA = 0, 512, 1024, 1536
OFF_CB, OFF_CC, OFF_CH, OFF_GC = 0, 512, 1024, 1536

SEQ_TILE = 512
KEY_TILE = 256
PV_ROWS = HEAD_DIM + BF16_ROWS
PAST_BLK = 256
VMEM_LIMIT = 58 * 1024 * 1024

BF16 = jnp.bfloat16
F32 = jnp.float32


def _dot(a, b):
    return jnp.dot(a, b, preferred_element_type=F32)


def _dot_nt(a, b):
    return lax.dot_general(a, b, (((1,), (1,)), ((), ())), preferred_element_type=F32)


def _split3(x):
    hi = x.astype(BF16)
    r = x - hi.astype(F32)
    mid = r.astype(BF16)
    lo = (r - mid.astype(F32)).astype(BF16)
    return hi, mid, lo


def _dot_exact_rhs(m01, x):
    hi, mid, lo = _split3(x)
    return _dot(m01, hi) + _dot(m01, mid) + _dot(m01, lo)


def _dot_exact_lhs(x, m01):
    hi, mid, lo = _split3(x)
    return _dot(hi, m01) + _dot(mid, m01) + _dot(lo, m01)


def _silu(x):
    return x / (1.0 + jnp.exp(-x))


def _log_sigmoid(x):
    return jnp.minimum(x, 0.0) - jnp.log1p(jnp.exp(-jnp.abs(x)))


def _lanes(rep, t):
    return jnp.concatenate([rep] * (t // LANES), axis=-1)


def _rms_t(xt, g_rep):
    t = xt.shape[1]
    ms = jnp.mean(xt * xt, axis=0, keepdims=True)
    return xt * lax.rsqrt(ms + EPS) * _lanes(g_rep, t)


def _head_rms_t(xt, g_rep, eps_row):
    t = xt.shape[1]
    x3 = xt.reshape(N_HEADS, HEAD_DIM, t)
    ms = jnp.mean(x3 * x3, axis=1, keepdims=True)
    return (x3 * lax.rsqrt(ms + eps_row[None]) * _lanes(g_rep, t)[None]).reshape(ATT_DIM, t)


def _shift_lanes(ut, prev, k):
    r = pltpu.roll(ut, k, axis=1)
    pr = pltpu.roll(prev, k, axis=1)
    lane = lax.broadcasted_iota(jnp.int32, prev.shape, 1)
    head = jnp.where(lane < k, pr, r[:, :LANES])
    return jnp.concatenate([head, r[:, LANES:]], axis=1)


def _prompt_kernel(n_aliased, x_ref, wa_ref, wf_ref, wb_ref, wo_ref, bf_ref, qg_ref, kg_ref, cw_ref, ag_ref, cg_ref,
                   up_ref, *refs):
    (y_ref, ko_ref, vo_ref, fo_ref, co_ref,
     kn_ref, vt_ref, ck_ref, qm_ref, z_ref, m_ref, al_ref, acc_ref, carry_ref, uprev_ref,
     gate_ref, zt_ref) = refs[n_aliased:]
    i = pl.program_id(1)
    nt = pl.num_programs(1)
    t = SEQ_TILE
    tk = KEY_TILE
    n_sub = t // tk
    base = n_sub * i

    @pl.when(i == 0)
    def _():
        carry_ref[...] = jnp.zeros_like(carry_ref)
        uprev_ref[...] = jnp.zeros_like(uprev_ref)

    x = x_ref[0]
    ht = x.astype(BF16).T
    ms = jnp.mean(x * x, axis=-1, keepdims=True) + EPS
    ms_row = jnp.broadcast_to(ms, (t, LANES)).T[:1]
    r_row = lax.rsqrt(ms_row)
    eps_row = EPS * ms_row

    def proj(w_ref, off, extra=None):
        half = ATT_DIM // 2
        second = w_ref[0, off + half:off + ATT_DIM, :]
        if extra is not None:
            second = jnp.concatenate([second, extra], axis=0)
        return jnp.concatenate([_dot(w_ref[0, off:off + half, :], ht), _dot(second, ht)], axis=0)

    def out_proj(a, row0):
        half = D_MODEL // 2
        return jnp.concatenate([_dot(a, wo_ref[0, row0:row0 + ATT_DIM, :half]),
                                _dot(a, wo_ref[0, row0:row0 + ATT_DIM, half:])], axis=1)

    kf = proj(wa_ref, OFF_K, wf_ref[0])
    logf = _log_sigmoid(kf[ATT_DIM:] * r_row + _lanes(bf_ref[...], t))
    fo_ref[0, 0] = logf[:N_HEADS]

    kt = _head_rms_t(kf[:ATT_DIM], kg_ref[...], eps_row)
    ko_ref[0, 0] = kt.reshape(N_HEADS, HEAD_DIM, t)
    kn = kt.astype(BF16).T
    for s in range(n_sub):
        kn_ref[base + s] = kn[s * tk:(s + 1) * tk]
    vt = proj(wa_ref, OFF_V) * r_row
    vo_ref[0, 0] = vt.reshape(N_HEADS, HEAD_DIM, t)
    ones_rows = jnp.ones((PV_ROWS - HEAD_DIM, tk), BF16)
    for hd in range(N_HEADS):
        v_h = vt[hd * HEAD_DIM:(hd + 1) * HEAD_DIM].astype(BF16)
        for s in range(n_sub):
            vt_ref[hd, base + s] = jnp.concatenate([v_h[:, s * tk:(s + 1) * tk], ones_rows], axis=0)

    qt = (_head_rms_t(proj(wa_ref, OFF_Q), qg_ref[...], eps_row) * (ATTN_SCALE * LOG2E)).astype(BF16)
    zero_rows = jnp.zeros((HEAD_DIM, t), BF16)
    for hd in range(N_HEADS):
        q_h = qt[hd * HEAD_DIM:(hd + 1) * HEAD_DIM]
        qm_ref[hd] = jnp.concatenate([q_h, zero_rows] if hd % 2 == 0 else [zero_rows, q_h], axis=0)

    lf3 = jnp.concatenate(_split3(logf), axis=0)
    c3 = _dot(jnp.concatenate([lf3[:, s * tk:(s + 1) * tk] for s in range(n_sub)], axis=0), up_ref[...])
    total = carry_ref[...]
    blocks = []
    for s in range(n_sub):
        cs = c3[3 * F_ROWS * s:3 * F_ROWS * (s + 1)]
        cs = (cs[:F_ROWS] + cs[F_ROWS:2 * F_ROWS] + cs[2 * F_ROWS:])[:N_HEADS] + _lanes(total, tk)
        total = jnp.broadcast_to(cs[:, tk - 1:], total.shape)
        blocks.append(cs)
    c = jnp.concatenate(blocks, axis=1)
    carry_ref[...] = total
    c_pos = jnp.concatenate([c, jnp.zeros((LANES - N_HEADS, t), F32)], axis=0).T
    for hd in range(N_HEADS):
        c_rep = jnp.broadcast_to(c_pos[:, hd:hd + 1] * LOG2E, (t, LANES))
        for s in range(n_sub):
            ck_ref[hd, base + s] = c_rep[s * tk:(s + 1) * tk]

    m_ref[...] = jnp.full(m_ref.shape, NEG, F32)
    acc_ref[...] = jnp.zeros_like(acc_ref)

    def track_max(hd, z, c0):
        z_ref[hd, :, c0:] = z
        m_old = m_ref[hd, :, c0:]
        m_new = jnp.maximum(m_old, jnp.max(z, axis=0, keepdims=True))
        al_ref[hd, :, c0:] = jnp.exp2(m_old - m_new)
        m_ref[hd, :, c0:] = m_new

    def scores_full(hd, j):
        p = hd // 2
        track_max(hd, _dot(kn_ref[j, :, p * LANES:(p + 1) * LANES], qm_ref[hd]) - _lanes(ck_ref[hd, j], t), 0)

    def scores_own(hd, s):
        p = hd // 2
        c0 = s * tk
        z = (_dot(kn_ref[base + s, :, p * LANES:(p + 1) * LANES], qm_ref[hd, :, c0:])
             - _lanes(ck_ref[hd, base + s], t - c0))
        keep = lax.broadcasted_iota(jnp.int32, (tk, tk), 0) <= lax.broadcasted_iota(jnp.int32, (tk, tk), 1)
        diag = jnp.where(keep, z[:, :tk], NEG)
        z = diag if t - c0 == tk else jnp.concatenate([diag, z[:, tk:]], axis=1)
        track_max(hd, z, c0)

    def accumulate(hd, j, c0=0):
        pr = jnp.exp2(z_ref[hd, :, c0:] - m_ref[hd, :, c0:]).astype(BF16)
        acc_ref[hd, :, c0:] = al_ref[hd, :, c0:] * acc_ref[hd, :, c0:] + _dot(vt_ref[hd, j], pr)

    ut = proj(wb_ref, OFF_CC) * proj(wb_ref, OFF_CH) * (r_row * r_row)
    prev = uprev_ref[...]
    yc = (_lanes(cw_ref[0], t) * _shift_lanes(ut, prev, 2) + _lanes(cw_ref[1], t) * _shift_lanes(ut, prev, 1)
          + _lanes(cw_ref[2], t) * ut)
    zn = _rms_t(proj(wb_ref, OFF_CB) * r_row * yc, cg_ref[...]) * _silu(proj(wb_ref, OFF_GC) * r_row)
    uprev_ref[...] = ut[:, t - LANES:]
    zt_ref[...] = zn.astype(BF16).T
    gate_ref[...] = _silu(proj(wa_ref, OFF_GA) * r_row)

    for hd in range(N_HEADS):
        scores_own(hd, n_sub - 1)
    for s in range(n_sub - 2, -1, -1):
        for hd in range(N_HEADS):
            accumulate(hd, base + s + 1, (s + 1) * tk)
            scores_own(hd, s)

    def body(g, carry):
        first = n_sub * g
        prev = jnp.where(g == 0, base, first - 1)
        for s in range(n_sub):
            for hd in range(N_HEADS):
                accumulate(hd, prev if s == 0 else first + s - 1)
                scores_full(hd, first + s)
        return carry

    lax.fori_loop(0, i, body, 0)
    last = jnp.where(i == 0, base, base - 1)
    quarter = D_MODEL // 4
    for hd in range(N_HEADS):
        accumulate(hd, last)
        if hd % 2 == 1:
            cs = slice((hd // 2) * quarter, (hd // 2 + 1) * quarter)
            y_ref[0, :, cs] = x_ref[0, :, cs] + _dot(zt_ref[...], wo_ref[0, ATT_DIM:, cs])

    outs = []
    for hd in range(N_HEADS):
        a = acc_ref[hd]
        outs.append(a[:HEAD_DIM] / a[HEAD_DIM:HEAD_DIM + 1])
    att = jnp.concatenate(outs, axis=0)
    an = (_rms_t(att, ag_ref[...]) * gate_ref[...]).astype(BF16).T
    y_ref[0] = y_ref[0] + out_proj(an, 0)

    @pl.when(i == nt - 1)
    def _():
        co_ref[0, 0] = uprev_ref[...].T[LANES - (CONV_W - 1):, :]


def _const_spec(shape):
    nd = len(shape)
    return pl.BlockSpec(shape, lambda *_: (0,) * nd)


def _layer_spec(a, layer):
    nd = a.ndim - 1
    return pl.BlockSpec((1,) + a.shape[1:], lambda *_: (layer,) + (0,) * nd)


def _prompt_layer(layer, depth, x, weights, consts, stacked):
    b, s, d = x.shape
    t = SEQ_TILE
    tk = KEY_TILE
    nt = s // t
    nk = s // tk
    any_spec = pl.BlockSpec(memory_space=pl.ANY)
    n_in = 1 + len(weights) + len(consts)
    return pl.pallas_call(
        functools.partial(_prompt_kernel, len(stacked)),
        grid=(b, nt),
        in_specs=[pl.BlockSpec((1, t, d), lambda bi, ti: (bi, ti, 0))] + [_layer_spec(a, layer) for a in weights]
                 + [_const_spec(a.shape) for a in consts] + [any_spec] * len(stacked),
        out_specs=[pl.BlockSpec((1, t, d), lambda bi, ti: (bi, ti, 0)),
                   pl.BlockSpec((1, 1, N_HEADS, HEAD_DIM, t), lambda bi, ti: (layer, bi, 0, 0, ti)),
                   pl.BlockSpec((1, 1, N_HEADS, HEAD_DIM, t), lambda bi, ti: (layer, bi, 0, 0, ti)),
                   pl.BlockSpec((1, 1, N_HEADS, t), lambda bi, ti: (layer, bi, 0, ti)),
                   pl.BlockSpec((1, 1, CONV_W - 1, CONV_DIM), lambda bi, ti: (layer, bi, 0, 0))],
        out_shape=[jax.ShapeDtypeStruct((b, s, d), F32),
                   jax.ShapeDtypeStruct((depth, b, N_HEADS, HEAD_DIM, s), F32),
                   jax.ShapeDtypeStruct((depth, b, N_HEADS, HEAD_DIM, s), F32),
                   jax.ShapeDtypeStruct((depth, b, N_HEADS, s), F32),
                   jax.ShapeDtypeStruct((depth, b, CONV_W - 1, CONV_DIM), F32)],
        input_output_aliases={n_in + a: 1 + a for a in range(len(stacked))},
        scratch_shapes=[
            pltpu.VMEM((nk, tk, ATT_DIM), BF16),
            pltpu.VMEM((N_HEADS, nk, PV_ROWS, tk), BF16),
            pltpu.VMEM((N_HEADS, nk, tk, LANES), F32),
            pltpu.VMEM((N_HEADS, LANES, t), BF16),
            pltpu.VMEM((N_HEADS, tk, t), F32),
            pltpu.VMEM((N_HEADS, 1, t), F32),
            pltpu.VMEM((N_HEADS, 1, t), F32),
            pltpu.VMEM((N_HEADS, PV_ROWS, t), F32),
            pltpu.VMEM((N_HEADS, LANES), F32),
            pltpu.VMEM((CONV_DIM, LANES), F32),
            pltpu.VMEM((ATT_DIM, t), F32),
            pltpu.VMEM((t, CONV_DIM), BF16),
        ],
        compiler_params=pltpu.CompilerParams(
            dimension_semantics=("arbitrary", "arbitrary"), vmem_limit_bytes=VMEM_LIMIT),
        name="prompt_layer",
    )(x, *weights, *consts, *stacked)


def _rms(x, g):
    ms = jnp.mean(x * x, axis=-1, keepdims=True)
    return x * lax.rsqrt(ms + EPS) * g


def _head_rms(x, g_tiled, bd):
    ss = _dot((x * x).astype(BF16), bd)
    return x * lax.rsqrt(ss * (1.0 / HEAD_DIM) + EPS) * g_tiled


def _shift_rows(u, prev8, k):
    r = pltpu.roll(u, k, axis=0)
    pr = pltpu.roll(prev8, k, axis=0)
    rowid = lax.broadcasted_iota(jnp.int32, prev8.shape, 0)
    head = jnp.where(rowid < k, pr, r[0:SUBLANES])
    return jnp.concatenate([head, r[SUBLANES:]], axis=0)


def _sample_kernel(x_ref, pk_ref, pv_ref, pf_ref, st_ref,
                   wa_ref, wf_ref, wb_ref, wo_ref, bf_ref, qg_ref, kg_ref, cw_ref, ag_ref, cg_ref,
                   bd_ref, tri_ref, upper_ref, ones_ref, blk_ref,
                   *refs):
    y_ref, ko_ref, vo_ref, fo_ref, co_ref = refs[-5:]
    n = x_ref.shape[1]
    n_blk = pf_ref.shape[1] // N_HEADS
    past = pk_ref.shape[-1]
    x = x_ref[0]
    h = _rms(x, 1.0).astype(BF16)
    bd = bd_ref[...]
    q = _head_rms(_dot(h, wa_ref[0, :, OFF_Q:OFF_Q + ATT_DIM]), qg_ref[...], bd) * (ATTN_SCALE * LOG2E)
    k = _head_rms(_dot(h, wa_ref[0, :, OFF_K:OFF_K + ATT_DIM]), kg_ref[...], bd)
    v = _dot(h, wa_ref[0, :, OFF_V:OFF_V + ATT_DIM])
    logf = _log_sigmoid(_dot(h, wf_ref[0]) + bf_ref[...])
    ko_ref[0, 0] = k
    vo_ref[0, 0] = v
    fo_ref[0, 0] = logf[:, :N_HEADS]

    pf = pf_ref[0]
    tot = _dot_exact_lhs(pf, ones_ref[...])
    before = _dot_exact_rhs(blk_ref[...], tot)
    c_past = _dot_exact_lhs(pf, upper_ref[...]) + before
    ck_past = jnp.concatenate(
        [c_past[bi * N_HEADS:(bi + 1) * N_HEADS] for bi in range(n_blk)], axis=1) * LOG2E
    past_total = (before + tot)[(n_blk - 1) * N_HEADS:, :LANES]

    zpad = jnp.zeros((LANES - n, F_PAD), F32)
    c_new = _dot_exact_rhs(tri_ref[...], jnp.concatenate([logf, zpad], axis=0))
    ck_new = (c_new.T[:N_HEADS, :] + past_total) * LOG2E

    row = lax.broadcasted_iota(jnp.int32, (n, LANES), 0)
    col = lax.broadcasted_iota(jnp.int32, (n, LANES), 1)
    keep = jnp.concatenate([col <= row] * 2, axis=0)
    low = lax.broadcasted_iota(jnp.int32, (n, LANES), 1) < HEAD_DIM
    cols = []
    for p in range(N_PAIRS):
        sl = slice(p * LANES, (p + 1) * LANES)
        kpt = pk_ref[0, 0, 2 * p:2 * p + 2].reshape(LANES, past).astype(BF16)
        vpt = pv_ref[0, 0, 2 * p:2 * p + 2].reshape(LANES, past).astype(BF16)
        zrows = jnp.zeros((LANES - n, LANES), F32)
        kn = jnp.concatenate([k[:, sl], zrows], axis=0).astype(BF16)
        vn = jnp.concatenate([v[:, sl], zrows], axis=0).astype(BF16)
        q_p = q[:, sl]
        qm = jnp.concatenate([jnp.where(low, q_p, 0.0), jnp.where(low, 0.0, q_p)], axis=0).astype(BF16)
        hd = 2 * p
        c_past = jnp.concatenate([jnp.broadcast_to(ck_past[hd + e:hd + e + 1, :], (n, past)) for e in (0, 1)], axis=0)
        c_new = jnp.concatenate([jnp.broadcast_to(ck_new[hd + e:hd + e + 1, :], (n, LANES)) for e in (0, 1)], axis=0)
        z_past = _dot(qm, kpt) - c_past
        z_new = jnp.where(keep, _dot_nt(qm, kn) - c_new, NEG)
        m = jnp.maximum(jnp.max(z_past, axis=-1, keepdims=True), jnp.max(z_new, axis=-1, keepdims=True))
        p_past = jnp.exp2(z_past - m)
        p_new = jnp.exp2(z_new - m)
        l = jnp.sum(p_past, axis=-1, keepdims=True) + jnp.sum(p_new, axis=-1, keepdims=True)
        o = (_dot_nt(p_past.astype(BF16), vpt) + _dot(p_new.astype(BF16), vn)) / l
        cols.append(jnp.where(low, o[:n], o[n:]))
    att = jnp.concatenate(cols, axis=1)

    st = st_ref[0, 0]
    u_prev8 = jnp.concatenate([jnp.zeros((SUBLANES - (CONV_W - 1), CONV_DIM), F32), st], axis=0)
    cc = _dot(h, wb_ref[0, :, OFF_CC:OFF_CC + CONV_DIM])
    ch = _dot(h, wb_ref[0, :, OFF_CH:OFF_CH + CONV_DIM])
    u = cc * ch
    cw = cw_ref[...]
    yc = cw[0:1] * _shift_rows(u, u_prev8, 2) + cw[1:2] * _shift_rows(u, u_prev8, 1) + cw[2:3] * u
    z = _dot(h, wb_ref[0, :, OFF_CB:OFF_CB + CONV_DIM]) * yc
    zn = _rms(z, cg_ref[...]) * _silu(_dot(h, wb_ref[0, :, OFF_GC:OFF_GC + CONV_DIM]))
    an = _rms(att, ag_ref[...]) * _silu(_dot(h, wa_ref[0, :, OFF_GA:OFF_GA + ATT_DIM]))
    mix = jnp.concatenate([an, zn], axis=1).astype(BF16)
    y_ref[0] = x + _dot(mix, wo_ref[0])
    co_ref[0, 0] = u[n - (CONV_W - 1):, :]


def _sample_layer(layer, depth, x, pk_t, pv_t, pf, st_all, weights, consts, stacked):
    b, n, d = x.shape
    per_b = lambda a: pl.BlockSpec((1,) + a.shape[1:], lambda bi: (bi,) + (0,) * (a.ndim - 1))
    per_lb = lambda a: pl.BlockSpec((1, 1) + a.shape[2:], lambda bi: (layer, bi) + (0,) * (a.ndim - 2))
    y_shape = jax.ShapeDtypeStruct((b, n, d), F32)
    stacked_shapes = [jax.ShapeDtypeStruct((depth, b, n, ATT_DIM), F32),
                      jax.ShapeDtypeStruct((depth, b, n, ATT_DIM), F32),
                      jax.ShapeDtypeStruct((depth, b, n, N_HEADS), F32),
                      jax.ShapeDtypeStruct((depth, b, CONV_W - 1, CONV_DIM), F32)]
    n_in = 5 + len(weights) + len(consts)
    return pl.pallas_call(
        _sample_kernel,
        grid=(b,),
        in_specs=[per_b(x), per_lb(pk_t), per_lb(pv_t), per_b(pf), per_lb(st_all)]
                 + [_layer_spec(a, layer) for a in weights] + [_const_spec(a.shape) for a in consts]
                 + [pl.BlockSpec(memory_space=pl.ANY)] * len(stacked),
        out_specs=[per_b(y_shape)] + [per_lb(a) for a in stacked_shapes],
        out_shape=[y_shape] + stacked_shapes,
        input_output_aliases={n_in + a: 1 + a for a in range(len(stacked))},
        compiler_params=pltpu.CompilerParams(
            dimension_semantics=("arbitrary",), vmem_limit_bytes=VMEM_LIMIT),
        name="sample_layer",
    )(x, pk_t, pv_t, pf, st_all, *weights, *consts, *stacked)


PREP_ROWS = 256


def _prep_kernel(w_ref, g_ref, o_ref, wa_ref, wf_ref, wb_ref, na_ref, nf_ref, nb_ref, wo_ref):
    g = g_ref[0]
    f0 = 4 * ATT_DIM
    for r0 in range(0, f0, PREP_ROWS):
        a = (w_ref[0, r0:r0 + PREP_ROWS, :] * g).astype(BF16)
        b = (w_ref[0, f0 + N_HEADS + r0:f0 + N_HEADS + r0 + PREP_ROWS, :] * g).astype(BF16)
        wa_ref[0, r0:r0 + PREP_ROWS, :] = a
        wb_ref[0, r0:r0 + PREP_ROWS, :] = b
        na_ref[0, :, r0:r0 + PREP_ROWS] = a.T
        nb_ref[0, :, r0:r0 + PREP_ROWS] = b.T
    wf = w_ref[0, f0:f0 + N_HEADS, :] * g
    wf = jnp.concatenate([wf, jnp.zeros((F_PAD - N_HEADS, D_MODEL), F32)], axis=0).astype(BF16)
    wf_ref[0] = wf[:F_ROWS]
    nf_ref[0] = wf.T
    for r0 in range(0, D_MODEL, PREP_ROWS):
        wo_ref[0, r0:r0 + PREP_ROWS, :] = o_ref[0, r0:r0 + PREP_ROWS, :].astype(BF16)


def _prep_weights(w_t, g, w_o):
    depth, n_in, d = w_t.shape
    n_half = 4 * ATT_DIM
    shapes = [(n_half, d), (F_ROWS, d), (n_half, d), (d, n_half), (d, F_PAD), (d, n_half), w_o.shape[1:]]
    per_l = lambda shape: pl.BlockSpec((1,) + shape, lambda l: (l, 0, 0), pipeline_mode=pl.Buffered(1))
    return pl.pallas_call(
        _prep_kernel,
        grid=(depth,),
        in_specs=[per_l((n_in, d)), per_l((1, d)), per_l(w_o.shape[1:])],
        out_specs=[per_l(sh) for sh in shapes],
        out_shape=[jax.ShapeDtypeStruct((depth,) + sh, BF16) for sh in shapes],
        compiler_params=pltpu.CompilerParams(
            dimension_semantics=("arbitrary",), vmem_limit_bytes=VMEM_LIMIT),
        name="prep_weights",
    )(w_t, g, w_o)


def _rep(v):
    return jnp.broadcast_to(v[..., None], v.shape + (LANES,))


def kernel(x_prompt, x_sample, cache_k, cache_v, cache_logf, state_conv, norm_g, w_in, b_f,
           q_norm_g, k_norm_g, conv_w, att_out_g, conv_out_g, w_out):
    depth = w_in.shape[0]
    bp, sp, _ = x_prompt.shape
    bs, ns, _ = x_sample.shape
    past = cache_k.shape[2]
    n_blk = past // PAST_BLK

    bd = jnp.asarray(np.kron(np.eye(N_HEADS, dtype=np.float32), np.ones((HEAD_DIM, HEAD_DIM), np.float32)), BF16)
    up_t = jnp.asarray(np.triu(np.ones((KEY_TILE, KEY_TILE), np.float32)), BF16)
    tri_l = jnp.asarray(np.tril(np.ones((LANES, LANES), np.float32)), BF16)
    upper = jnp.asarray(np.triu(np.ones((PAST_BLK, PAST_BLK), np.float32)), BF16)
    ones = jnp.ones((PAST_BLK, PAST_BLK), BF16)
    r = np.arange(n_blk * N_HEADS)
    blk = jnp.asarray(((r[:, None] % N_HEADS == r[None, :] % N_HEADS)
                       & (r[None, :] // N_HEADS < r[:, None] // N_HEADS)).astype(np.float32), BF16)

    pk_t = jnp.transpose(cache_k, (0, 1, 3, 4, 2))
    pv_t = jnp.transpose(cache_v, (0, 1, 3, 4, 2))

    *weights, wo_all = _prep_weights(jnp.transpose(w_in, (0, 2, 1)), norm_g[:, None, :], w_out)
    stacked, stacked_s = (), ()
    yp, ys = x_prompt, x_sample
    for l in range(depth):
        bf_rows = _rep(jnp.concatenate([b_f[l], jnp.zeros((F_ROWS - N_HEADS,), F32)]))

        p_consts = (bf_rows, _rep(q_norm_g[l]), _rep(k_norm_g[l]), _rep(conv_w[l]),
                    _rep(att_out_g[l]), _rep(conv_out_g[l]), up_t)
        yp, *stacked = _prompt_layer(l, depth, yp, weights[:3] + [wo_all], p_consts, tuple(stacked))

        bfp = jnp.concatenate([b_f[l], jnp.zeros((F_PAD - N_HEADS,), F32)])[None, :]
        s_consts = (bfp, jnp.tile(q_norm_g[l], N_HEADS)[None, :], jnp.tile(k_norm_g[l], N_HEADS)[None, :],
                    conv_w[l], att_out_g[l][None, :], conv_out_g[l][None, :], bd, tri_l, upper, ones, blk)
        pf = cache_logf[l].reshape(bs, n_blk, PAST_BLK, N_HEADS).transpose(0, 1, 3, 2)
        pf = pf.reshape(bs, n_blk * N_HEADS, PAST_BLK)
        ys, *stacked_s = _sample_layer(l, depth, ys, pk_t, pv_t, pf, state_conv, weights[3:] + [wo_all], s_consts,
                                       tuple(stacked_s))
    k_all, v_all, f_all, c_prompt = stacked
    k_s, v_s, f_s, c_s = stacked_s
    k_prompt = jnp.transpose(k_all, (0, 1, 4, 2, 3))
    v_prompt = jnp.transpose(v_all, (0, 1, 4, 2, 3))
    f_prompt = jnp.transpose(f_all, (0, 1, 3, 2))
    k_s = k_s.reshape(depth, bs, ns, N_HEADS, HEAD_DIM)
    v_s = v_s.reshape(depth, bs, ns, N_HEADS, HEAD_DIM)
    return (yp, ys, k_prompt, v_prompt, f_prompt, c_prompt, k_s, v_s, f_s, c_s)
```

```python
import functools

import numpy as np
import jax
import jax.numpy as jnp
from jax import lax
from jax.experimental import pallas as pl
from jax.experimental.pallas import tpu as pltpu

D_MODEL = 1024
HEAD_DIM = 64
N_HEADS = 8
N_PAIRS = N_HEADS // 2
ATT_DIM = N_HEADS * HEAD_DIM
CONV_DIM = 512
CONV_W = 3
EPS = 1e-6
NEG = -1e30
ATTN_SCALE = HEAD_DIM ** -0.5
LOG2E = 1.4426950408889634

LANES = 128
SUBLANES = 8
BF16_ROWS = 16
F_PAD = LANES
F_ROWS = BF16_ROWS
OFF_Q, OFF_K, OFF_V, OFF_GA = 0, 512, 1024, 1536
OFF_CB, OFF_CC, OFF_CH, OFF_GC = 0, 512, 1024, 1536

SEQ_TILE = 512
KEY_TILE = 256
PV_ROWS = HEAD_DIM + BF16_ROWS
PAST_BLK = 256
SAMPLE_STREAMS = 2
VMEM_LIMIT = 58 * 1024 * 1024

BF16 = jnp.bfloat16
F32 = jnp.float32


def _dot(a, b):
    return jnp.dot(a, b, preferred_element_type=F32)


def _dot_nt(a, b):
    return lax.dot_general(a, b, (((1,), (1,)), ((), ())), preferred_element_type=F32)


def _split3(x):
    hi = x.astype(BF16)
    r = x - hi.astype(F32)
    mid = r.astype(BF16)
    lo = (r - mid.astype(F32)).astype(BF16)
    return hi, mid, lo


def _dot_exact_rhs(m01, x):
    hi, mid, lo = _split3(x)
    return _dot(m01, hi) + _dot(m01, mid) + _dot(m01, lo)


def _dot_exact_lhs(x, m01):
    hi, mid, lo = _split3(x)
    return _dot(hi, m01) + _dot(mid, m01) + _dot(lo, m01)


def _silu(x):
    return x / (1.0 + jnp.exp(-x))


def _log_sigmoid(x):
    return jnp.minimum(x, 0.0) - jnp.log1p(jnp.exp(-jnp.abs(x)))


def _lanes(rep, t):
    return jnp.concatenate([rep] * (t // LANES), axis=-1)


def _rms_t(xt, g_rep):
    t = xt.shape[1]
    ms = jnp.mean(xt * xt, axis=0, keepdims=True)
    return xt * lax.rsqrt(ms + EPS) * _lanes(g_rep, t)


def _head_rms_t(xt, g_rep, eps_row):
    t = xt.shape[1]
    x3 = xt.reshape(N_HEADS, HEAD_DIM, t)
    ms = jnp.mean(x3 * x3, axis=1, keepdims=True)
    return (x3 * lax.rsqrt(ms + eps_row[None]) * _lanes(g_rep, t)[None]).reshape(ATT_DIM, t)


def _shift_lanes(ut, prev, k):
    r = pltpu.roll(ut, k, axis=1)
    pr = pltpu.roll(prev, k, axis=1)
    lane = lax.broadcasted_iota(jnp.int32, prev.shape, 1)
    head = jnp.where(lane < k, pr, r[:, :LANES])
    return jnp.concatenate([head, r[:, LANES:]], axis=1)


def _prompt_kernel(n_aliased, x_ref, wa_ref, wf_ref, wb_ref, wo_ref, bf_ref, qg_ref, kg_ref, cw_ref, ag_ref, cg_ref,
                   up_ref, *refs):
    (y_ref, ko_ref, vo_ref, fo_ref, co_ref,
     kn_ref, vt_ref, ck_ref, qm_ref, z_ref, m_ref, al_ref, acc_ref, carry_ref, uprev_ref,
     gate_ref, zt_ref) = refs[n_aliased:]
    i = pl.program_id(1)
    nt = pl.num_programs(1)
    t = SEQ_TILE
    tk = KEY_TILE
    n_sub = t // tk
    base = n_sub * i

    @pl.when(i == 0)
    def _():
        carry_ref[...] = jnp.zeros_like(carry_ref)
        uprev_ref[...] = jnp.zeros_like(uprev_ref)

    x = x_ref[0]
    ht = x.astype(BF16).T
    ms = jnp.mean(x * x, axis=-1, keepdims=True) + EPS
    ms_row = jnp.broadcast_to(ms, (t, LANES)).T[:1]
    r_row = lax.rsqrt(ms_row)
    eps_row = EPS * ms_row

    def proj(w_ref, off, extra=None):
        half = ATT_DIM // 2
        second = w_ref[0, off + half:off + ATT_DIM, :]
        if extra is not None:
            second = jnp.concatenate([second, extra], axis=0)
        return jnp.concatenate([_dot(w_ref[0, off:off + half, :], ht), _dot(second, ht)], axis=0)

    def out_proj(a, row0):
        half = D_MODEL // 2
        return jnp.concatenate([_dot(a, wo_ref[0, row0:row0 + ATT_DIM, :half]),
                                _dot(a, wo_ref[0, row0:row0 + ATT_DIM, half:])], axis=1)

    kf = proj(wa_ref, OFF_K, wf_ref[0])
    logf = _log_sigmoid(kf[ATT_DIM:] * r_row + _lanes(bf_ref[...], t))
    fo_ref[0, 0] = logf[:N_HEADS]

    kt = _head_rms_t(kf[:ATT_DIM], kg_ref[...], eps_row)
    ko_ref[0, 0] = kt.reshape(N_HEADS, HEAD_DIM, t)
    kn = kt.astype(BF16).T
    for s in range(n_sub):
        kn_ref[base + s] = kn[s * tk:(s + 1) * tk]
    vt = proj(wa_ref, OFF_V) * r_row
    vo_ref[0, 0] = vt.reshape(N_HEADS, HEAD_DIM, t)
    ones_rows = jnp.ones((PV_ROWS - HEAD_DIM, tk), BF16)
    for hd in range(N_HEADS):
        v_h = vt[hd * HEAD_DIM:(hd + 1) * HEAD_DIM].astype(BF16)
        for s in range(n_sub):
            vt_ref[hd, base + s] = jnp.concatenate([v_h[:, s * tk:(s + 1) * tk], ones_rows], axis=0)

    qt = (_head_rms_t(proj(wa_ref, OFF_Q), qg_ref[...], eps_row) * (ATTN_SCALE * LOG2E)).astype(BF16)
    zero_rows = jnp.zeros((HEAD_DIM, t), BF16)
    for hd in range(N_HEADS):
        q_h = qt[hd * HEAD_DIM:(hd + 1) * HEAD_DIM]
        qm_ref[hd] = jnp.concatenate([q_h, zero_rows] if hd % 2 == 0 else [zero_rows, q_h], axis=0)

    lf3 = jnp.concatenate(_split3(logf), axis=0)
    c3 = _dot(jnp.concatenate([lf3[:, s * tk:(s + 1) * tk] for s in range(n_sub)], axis=0), up_ref[...])
    total = carry_ref[...]
    blocks = []
    for s in range(n_sub):
        cs = c3[3 * F_ROWS * s:3 * F_ROWS * (s + 1)]
        cs = (cs[:F_ROWS] + cs[F_ROWS:2 * F_ROWS] + cs[2 * F_ROWS:])[:N_HEADS] + _lanes(total, tk)
        total = jnp.broadcast_to(cs[:, tk - 1:], total.shape)
        blocks.append(cs)
    c = jnp.concatenate(blocks, axis=1)
    carry_ref[...] = total
    c_pos = jnp.concatenate([c, jnp.zeros((LANES - N_HEADS, t), F32)], axis=0).T
    for hd in range(N_HEADS):
        c_rep = jnp.broadcast_to(c_pos[:, hd:hd + 1] * LOG2E, (t, LANES))
        for s in range(n_sub):
            ck_ref[hd, base + s] = c_rep[s * tk:(s + 1) * tk]

    m_ref[...] = jnp.full(m_ref.shape, NEG, F32)
    acc_ref[...] = jnp.zeros_like(acc_ref)

    def track_max(hd, z, c0):
        z_ref[hd, :, c0:] = z
        m_old = m_ref[hd, :, c0:]
        m_new = jnp.maximum(m_old, jnp.max(z, axis=0, keepdims=True))
        al_ref[hd, :, c0:] = jnp.exp2(m_old - m_new)
        m_ref[hd, :, c0:] = m_new

    def scores_full(hd, j):
        p = hd // 2
        track_max(hd, _dot(kn_ref[j, :, p * LANES:(p + 1) * LANES], qm_ref[hd]) - _lanes(ck_ref[hd, j], t), 0)

    def scores_own(hd, s):
        p = hd // 2
        c0 = s * tk
        z = (_dot(kn_ref[base + s, :, p * LANES:(p + 1) * LANES], qm_ref[hd, :, c0:])
             - _lanes(ck_ref[hd, base + s], t - c0))
        keep = lax.broadcasted_iota(jnp.int32, (tk, tk), 0) <= lax.broadcasted_iota(jnp.int32, (tk, tk), 1)
        diag = jnp.where(keep, z[:, :tk], NEG)
        z = diag if t - c0 == tk else jnp.concatenate([diag, z[:, tk:]], axis=1)
        track_max(hd, z, c0)

    def accumulate(hd, j, c0=0):
        pr = jnp.exp2(z_ref[hd, :, c0:] - m_ref[hd, :, c0:]).astype(BF16)
        acc_ref[hd, :, c0:] = al_ref[hd, :, c0:] * acc_ref[hd, :, c0:] + _dot(vt_ref[hd, j], pr)

    ut = proj(wb_ref, OFF_CC) * proj(wb_ref, OFF_CH) * (r_row * r_row)
    prev = uprev_ref[...]
    yc = (_lanes(cw_ref[0], t) * _shift_lanes(ut, prev, 2) + _lanes(cw_ref[1], t) * _shift_lanes(ut, prev, 1)
          + _lanes(cw_ref[2], t) * ut)
    zn = _rms_t(proj(wb_ref, OFF_CB) * r_row * yc, cg_ref[...]) * _silu(proj(wb_ref, OFF_GC) * r_row)
    uprev_ref[...] = ut[:, t - LANES:]
    zt_ref[...] = zn.astype(BF16).T
    gate_ref[...] = _silu(proj(wa_ref, OFF_GA) * r_row)

    for hd in range(N_HEADS):
        scores_own(hd, n_sub - 1)
    for s in range(n_sub - 2, -1, -1):
        for hd in range(N_HEADS):
            accumulate(hd, base + s + 1, (s + 1) * tk)
            scores_own(hd, s)

    def body(g, carry):
        first = n_sub * g
        prev = jnp.where(g == 0, base, first - 1)
        for s in range(n_sub):
            for hd in range(N_HEADS):
                accumulate(hd, prev if s == 0 else first + s - 1)
                scores_full(hd, first + s)
        return carry

    lax.fori_loop(0, i, body, 0)
    last = jnp.where(i == 0, base, base - 1)
    quarter = D_MODEL // 4
    for hd in range(N_HEADS):
        accumulate(hd, last)
        if hd % 2 == 1:
            cs = slice((hd // 2) * quarter, (hd // 2 + 1) * quarter)
            y_ref[0, :, cs] = x_ref[0, :, cs] + _dot(zt_ref[...], wo_ref[0, ATT_DIM:, cs])

    outs = []
    for hd in range(N_HEADS):
        a = acc_ref[hd]
        outs.append(a[:HEAD_DIM] / a[HEAD_DIM:HEAD_DIM + 1])
    att = jnp.concatenate(outs, axis=0)
    an = (_rms_t(att, ag_ref[...]) * gate_ref[...]).astype(BF16).T
    y_ref[0] = y_ref[0] + out_proj(an, 0)

    @pl.when(i == nt - 1)
    def _():
        co_ref[0, 0] = uprev_ref[...].T[LANES - (CONV_W - 1):, :]


def _const_spec(shape):
    nd = len(shape)
    return pl.BlockSpec(shape, lambda *_: (0,) * nd)


def _layer_spec(a, layer):
    nd = a.ndim - 1
    return pl.BlockSpec((1,) + a.shape[1:], lambda *_: (layer,) + (0,) * nd)


def _prompt_layer(layer, depth, x, weights, consts, stacked):
    b, s, d = x.shape
    t = SEQ_TILE
    tk = KEY_TILE
    nt = s // t
    nk = s // tk
    any_spec = pl.BlockSpec(memory_space=pl.ANY)
    n_in = 1 + len(weights) + len(consts)
    return pl.pallas_call(
        functools.partial(_prompt_kernel, len(stacked)),
        grid=(b, nt),
        in_specs=[pl.BlockSpec((1, t, d), lambda bi, ti: (bi, ti, 0))] + [_layer_spec(a, layer) for a in weights]
                 + [_const_spec(a.shape) for a in consts] + [any_spec] * len(stacked),
        out_specs=[pl.BlockSpec((1, t, d), lambda bi, ti: (bi, ti, 0)),
                   pl.BlockSpec((1, 1, N_HEADS, HEAD_DIM, t), lambda bi, ti: (layer, bi, 0, 0, ti)),
                   pl.BlockSpec((1, 1, N_HEADS, HEAD_DIM, t), lambda bi, ti: (layer, bi, 0, 0, ti)),
                   pl.BlockSpec((1, 1, N_HEADS, t), lambda bi, ti: (layer, bi, 0, ti)),
                   pl.BlockSpec((1, 1, CONV_W - 1, CONV_DIM), lambda bi, ti: (layer, bi, 0, 0))],
        out_shape=[jax.ShapeDtypeStruct((b, s, d), F32),
                   jax.ShapeDtypeStruct((depth, b, N_HEADS, HEAD_DIM, s), F32),
                   jax.ShapeDtypeStruct((depth, b, N_HEADS, HEAD_DIM, s), F32),
                   jax.ShapeDtypeStruct((depth, b, N_HEADS, s), F32),
                   jax.ShapeDtypeStruct((depth, b, CONV_W - 1, CONV_DIM), F32)],
        input_output_aliases={n_in + a: 1 + a for a in range(len(stacked))},
        scratch_shapes=[
            pltpu.VMEM((nk, tk, ATT_DIM), BF16),
            pltpu.VMEM((N_HEADS, nk, PV_ROWS, tk), BF16),
            pltpu.VMEM((N_HEADS, nk, tk, LANES), F32),
            pltpu.VMEM((N_HEADS, LANES, t), BF16),
            pltpu.VMEM((N_HEADS, tk, t), F32),
            pltpu.VMEM((N_HEADS, 1, t), F32),
            pltpu.VMEM((N_HEADS, 1, t), F32),
            pltpu.VMEM((N_HEADS, PV_ROWS, t), F32),
            pltpu.VMEM((N_HEADS, LANES), F32),
            pltpu.VMEM((CONV_DIM, LANES), F32),
            pltpu.VMEM((ATT_DIM, t), F32),
            pltpu.VMEM((t, CONV_DIM), BF16),
        ],
        compiler_params=pltpu.CompilerParams(
            dimension_semantics=("arbitrary", "arbitrary"), vmem_limit_bytes=VMEM_LIMIT),
        name="prompt_layer",
    )(x, *weights, *consts, *stacked)


def _rms(x, g):
    ms = jnp.mean(x * x, axis=-1, keepdims=True)
    return x * lax.rsqrt(ms + EPS) * g


def _head_rms(x, g_tiled, bd):
    ss = _dot((x * x).astype(BF16), bd)
    return x * lax.rsqrt(ss * (1.0 / HEAD_DIM) + EPS) * g_tiled


def _shift_rows(u, prev8, k):
    r = pltpu.roll(u, k, axis=0)
    pr = pltpu.roll(prev8, k, axis=0)
    rowid = lax.broadcasted_iota(jnp.int32, prev8.shape, 0)
    head = jnp.where(rowid < k, pr, r[0:SUBLANES])
    return jnp.concatenate([head, r[SUBLANES:]], axis=0)


def _sample_kernel(x_ref, pk_ref, pv_ref, pf_ref, st_ref,
                   wa_ref, wf_ref, wb_ref, wo_ref, bf_ref, qg_ref, kg_ref, cw_ref, ag_ref, cg_ref,
                   bd_ref, tri_ref, upper_ref, ones_ref, blk_ref,
                   *refs):
    y_ref, ko_ref, vo_ref, fo_ref, co_ref = refs[-5:]
    ns, n = x_ref.shape[0], x_ref.shape[1]
    n_blk = pf_ref.shape[1] // N_HEADS
    past = pk_ref.shape[-1]
    x = x_ref[...].reshape(ns * n, D_MODEL)
    h = _rms(x, 1.0).astype(BF16)
    bd = bd_ref[...]
    q = _head_rms(_dot(h, wa_ref[0, :, OFF_Q:OFF_Q + ATT_DIM]), qg_ref[...], bd) * (ATTN_SCALE * LOG2E)
    k = _head_rms(_dot(h, wa_ref[0, :, OFF_K:OFF_K + ATT_DIM]), kg_ref[...], bd)
    v = _dot(h, wa_ref[0, :, OFF_V:OFF_V + ATT_DIM])
    logf = _log_sigmoid(_dot(h, wf_ref[0]) + bf_ref[...])
    ko_ref[0] = k.reshape(ns, n, ATT_DIM)
    vo_ref[0] = v.reshape(ns, n, ATT_DIM)
    fo_ref[0] = logf[:, :N_HEADS].reshape(ns, n, N_HEADS)

    row = lax.broadcasted_iota(jnp.int32, (n, LANES), 0)
    col = lax.broadcasted_iota(jnp.int32, (n, LANES), 1)
    keep = jnp.concatenate([col <= row] * 2, axis=0)
    low = lax.broadcasted_iota(jnp.int32, (n, LANES), 1) < HEAD_DIM

    def attend(sb):
        rs = slice(sb * n, (sb + 1) * n)
        pf = pf_ref[sb]
        tot = _dot_exact_lhs(pf, ones_ref[...])
        before = _dot_exact_rhs(blk_ref[...], tot)
        c_blk = _dot_exact_lhs(pf, upper_ref[...]) + before
        ck_past = jnp.concatenate(
            [c_blk[bi * N_HEADS:(bi + 1) * N_HEADS] for bi in range(n_blk)], axis=1) * LOG2E
        past_total = (before + tot)[(n_blk - 1) * N_HEADS:, :LANES]
        zpad = jnp.zeros((LANES - n, F_PAD), F32)
        c_pos = _dot_exact_rhs(tri_ref[...], jnp.concatenate([logf[rs], zpad], axis=0))
        ck_new = (c_pos.T[:N_HEADS, :] + past_total) * LOG2E

        cols = []
        for p in range(N_PAIRS):
            sl = slice(p * LANES, (p + 1) * LANES)
            kpt = pk_ref[0, sb, 2 * p:2 * p + 2].reshape(LANES, past).astype(BF16)
            vpt = pv_ref[0, sb, 2 * p:2 * p + 2].reshape(LANES, past).astype(BF16)
            zrows = jnp.zeros((LANES - n, LANES), F32)
            kn = jnp.concatenate([k[rs, sl], zrows], axis=0).astype(BF16)
            vn = jnp.concatenate([v[rs, sl], zrows], axis=0).astype(BF16)
            q_p = q[rs, sl]
            qm = jnp.concatenate([jnp.where(low, q_p, 0.0), jnp.where(low, 0.0, q_p)], axis=0).astype(BF16)
            hd = 2 * p
            c_past = jnp.concatenate(
                [jnp.broadcast_to(ck_past[hd + e:hd + e + 1, :], (n, past)) for e in (0, 1)], axis=0)
            c_new = jnp.concatenate(
                [jnp.broadcast_to(ck_new[hd + e:hd + e + 1, :], (n, LANES)) for e in (0, 1)], axis=0)
            z_past = _dot(qm, kpt) - c_past
            z_new = jnp.where(keep, _dot_nt(qm, kn) - c_new, NEG)
            m = jnp.maximum(jnp.max(z_past, axis=-1, keepdims=True), jnp.max(z_new, axis=-1, keepdims=True))
            p_past = jnp.exp2(z_past - m)
            p_new = jnp.exp2(z_new - m)
            l = jnp.sum(p_past, axis=-1, keepdims=True) + jnp.sum(p_new, axis=-1, keepdims=True)
            o = (_dot_nt(p_past.astype(BF16), vpt) + _dot(p_new.astype(BF16), vn)) / l
            cols.append(jnp.where(low, o[:n], o[n:]))
        return jnp.concatenate(cols, axis=1)

    att = jnp.concatenate([attend(sb) for sb in range(ns)], axis=0)

    cc = _dot(h, wb_ref[0, :, OFF_CC:OFF_CC + CONV_DIM])
    ch = _dot(h, wb_ref[0, :, OFF_CH:OFF_CH + CONV_DIM])
    u = cc * ch
    cw = cw_ref[...]
    ycs = []
    for sb in range(ns):
        u_s = u[sb * n:(sb + 1) * n]
        u_prev8 = jnp.concatenate([jnp.zeros((SUBLANES - (CONV_W - 1), CONV_DIM), F32), st_ref[0, sb]], axis=0)
        ycs.append(cw[0:1] * _shift_rows(u_s, u_prev8, 2) + cw[1:2] * _shift_rows(u_s, u_prev8, 1) + cw[2:3] * u_s)
        co_ref[0, sb] = u_s[n - (CONV_W - 1):, :]
    z = _dot(h, wb_ref[0, :, OFF_CB:OFF_CB + CONV_DIM]) * jnp.concatenate(ycs, axis=0)
    zn = _rms(z, cg_ref[...]) * _silu(_dot(h, wb_ref[0, :, OFF_GC:OFF_GC + CONV_DIM]))
    an = _rms(att, ag_ref[...]) * _silu(_dot(h, wa_ref[0, :, OFF_GA:OFF_GA + ATT_DIM]))
    mix = jnp.concatenate([an, zn], axis=1).astype(BF16)
    y_ref[...] = (x + _dot(mix, wo_ref[0])).reshape(ns, n, D_MODEL)


def _sample_layer(layer, depth, x, pk_t, pv_t, pf, st_all, weights, consts, stacked):
    b, n, d = x.shape
    ns = SAMPLE_STREAMS
    per_b = lambda a: pl.BlockSpec((ns,) + a.shape[1:], lambda bi: (bi,) + (0,) * (a.ndim - 1))
    per_lb = lambda a: pl.BlockSpec((1, ns) + a.shape[2:], lambda bi: (layer, bi) + (0,) * (a.ndim - 2))
    y_shape = jax.ShapeDtypeStruct((b, n, d), F32)
    stacked_shapes = [jax.ShapeDtypeStruct((depth, b, n, ATT_DIM), F32),
                      jax.ShapeDtypeStruct((depth, b, n, ATT_DIM), F32),
                      jax.ShapeDtypeStruct((depth, b, n, N_HEADS), F32),
                      jax.ShapeDtypeStruct((depth, b, CONV_W - 1, CONV_DIM), F32)]
    n_in = 5 + len(weights) + len(consts)
    return pl.pallas_call(
        _sample_kernel,
        grid=(b // ns,),
        in_specs=[per_b(x), per_lb(pk_t), per_lb(pv_t), per_b(pf), per_lb(st_all)]
                 + [_layer_spec(a, layer) for a in weights] + [_const_spec(a.shape) for a in consts]
                 + [pl.BlockSpec(memory_space=pl.ANY)] * len(stacked),
        out_specs=[per_b(y_shape)] + [per_lb(a) for a in stacked_shapes],
        out_shape=[y_shape] + stacked_shapes,
        input_output_aliases={n_in + a: 1 + a for a in range(len(stacked))},
        compiler_params=pltpu.CompilerParams(
            dimension_semantics=("arbitrary",), vmem_limit_bytes=VMEM_LIMIT),
        name="sample_layer",
    )(x, pk_t, pv_t, pf, st_all, *weights, *consts, *stacked)


PREP_ROWS = 256


def _prep_kernel(w_ref, g_ref, o_ref, wa_ref, wf_ref, wb_ref, na_ref, nf_ref, nb_ref, wo_ref):
    g = g_ref[0]
    f0 = 4 * ATT_DIM
    for r0 in range(0, f0, PREP_ROWS):
        a = (w_ref[0, r0:r0 + PREP_ROWS, :] * g).astype(BF16)
        b = (w_ref[0, f0 + N_HEADS + r0:f0 + N_HEADS + r0 + PREP_ROWS, :] * g).astype(BF16)
        wa_ref[0, r0:r0 + PREP_ROWS, :] = a
        wb_ref[0, r0:r0 + PREP_ROWS, :] = b
        na_ref[0, :, r0:r0 + PREP_ROWS] = a.T
        nb_ref[0, :, r0:r0 + PREP_ROWS] = b.T
    wf = w_ref[0, f0:f0 + N_HEADS, :] * g
    wf = jnp.concatenate([wf, jnp.zeros((F_PAD - N_HEADS, D_MODEL), F32)], axis=0).astype(BF16)
    wf_ref[0] = wf[:F_ROWS]
    nf_ref[0] = wf.T
    for r0 in range(0, D_MODEL, PREP_ROWS):
        wo_ref[0, r0:r0 + PREP_ROWS, :] = o_ref[0, r0:r0 + PREP_ROWS, :].astype(BF16)


def _prep_weights(w_t, g, w_o):
    depth, n_in, d = w_t.shape
    n_half = 4 * ATT_DIM
    shapes = [(n_half, d), (F_ROWS, d), (n_half, d), (d, n_half), (d, F_PAD), (d, n_half), w_o.shape[1:]]
    per_l = lambda shape: pl.BlockSpec((1,) + shape, lambda l: (l, 0, 0), pipeline_mode=pl.Buffered(1))
    return pl.pallas_call(
        _prep_kernel,
        grid=(depth,),
        in_specs=[per_l((n_in, d)), per_l((1, d)), per_l(w_o.shape[1:])],
        out_specs=[per_l(sh) for sh in shapes],
        out_shape=[jax.ShapeDtypeStruct((depth,) + sh, BF16) for sh in shapes],
        compiler_params=pltpu.CompilerParams(
            dimension_semantics=("arbitrary",), vmem_limit_bytes=VMEM_LIMIT),
        name="prep_weights",
    )(w_t, g, w_o)


def _rep(v):
    return jnp.broadcast_to(v[..., None], v.shape + (LANES,))


def kernel(x_prompt, x_sample, cache_k, cache_v, cache_logf, state_conv, norm_g, w_in, b_f,
           q_norm_g, k_norm_g, conv_w, att_out_g, conv_out_g, w_out):
    depth = w_in.shape[0]
    bp, sp, _ = x_prompt.shape
    bs, ns, _ = x_sample.shape
    past = cache_k.shape[2]
    n_blk = past // PAST_BLK

    bd = jnp.asarray(np.kron(np.eye(N_HEADS, dtype=np.float32), np.ones((HEAD_DIM, HEAD_DIM), np.float32)), BF16)
    up_t = jnp.asarray(np.triu(np.ones((KEY_TILE, KEY_TILE), np.float32)), BF16)
    tri_l = jnp.asarray(np.tril(np.ones((LANES, LANES), np.float32)), BF16)
    upper = jnp.asarray(np.triu(np.ones((PAST_BLK, PAST_BLK), np.float32)), BF16)
    ones = jnp.ones((PAST_BLK, PAST_BLK), BF16)
    r = np.arange(n_blk * N_HEADS)
    blk = jnp.asarray(((r[:, None] % N_HEADS == r[None, :] % N_HEADS)
                       & (r[None, :] // N_HEADS < r[:, None] // N_HEADS)).astype(np.float32), BF16)

    pk_t = jnp.transpose(cache_k, (0, 1, 3, 4, 2))
    pv_t = jnp.transpose(cache_v, (0, 1, 3, 4, 2))

    *weights, wo_all = _prep_weights(jnp.transpose(w_in, (0, 2, 1)), norm_g[:, None, :], w_out)
    stacked, stacked_s = (), ()
    yp, ys = x_prompt, x_sample
    for l in range(depth):
        bf_rows = _rep(jnp.concatenate([b_f[l], jnp.zeros((F_ROWS - N_HEADS,), F32)]))

        p_consts = (bf_rows, _rep(q_norm_g[l]), _rep(k_norm_g[l]), _rep(conv_w[l]),
                    _rep(att_out_g[l]), _rep(conv_out_g[l]), up_t)
        yp, *stacked = _prompt_layer(l, depth, yp, weights[:3] + [wo_all], p_consts, tuple(stacked))

        bfp = jnp.concatenate([b_f[l], jnp.zeros((F_PAD - N_HEADS,), F32)])[None, :]
        s_consts = (bfp, jnp.tile(q_norm_g[l], N_HEADS)[None, :], jnp.tile(k_norm_g[l], N_HEADS)[None, :],
                    conv_w[l], att_out_g[l][None, :], conv_out_g[l][None, :], bd, tri_l, upper, ones, blk)
        pf = cache_logf[l].reshape(bs, n_blk, PAST_BLK, N_HEADS).transpose(0, 1, 3, 2)
        pf = pf.reshape(bs, n_blk * N_HEADS, PAST_BLK)
        ys, *stacked_s = _sample_layer(l, depth, ys, pk_t, pv_t, pf, state_conv, weights[3:] + [wo_all], s_consts,
                                       tuple(stacked_s))
    k_all, v_all, f_all, c_prompt = stacked
    k_s, v_s, f_s, c_s = stacked_s
    k_prompt = jnp.transpose(k_all, (0, 1, 4, 2, 3))
    v_prompt = jnp.transpose(v_all, (0, 1, 4, 2, 3))
    f_prompt = jnp.transpose(f_all, (0, 1, 3, 2))
    k_s = k_s.reshape(depth, bs, ns, N_HEADS, HEAD_DIM)
    v_s = v_s.reshape(depth, bs, ns, N_HEADS, HEAD_DIM)
    return (yp, ys, k_prompt, v_prompt, f_prompt, c_prompt, k_s, v_s, f_s, c_s)
```
